```python
import jax, jax.numpy as jnp
from jax import lax
import numpy as np

D_MODEL = 4096
BATCH = 2
SEQ = 4096
DEPTH = 2
DEC_BATCH = 4
DEC_SEQ = 4096
PAST_LEN = 128

GRID_W = 64
N_MIXERS = 2
N_CONV_LAYERS = (DEPTH + 1) // 2
N_ATTN_LAYERS = DEPTH // 2
CONV_WIDTH = 31
NA_HEAD_DIM = 128
NA_HEADS = D_MODEL // NA_HEAD_DIM
WIN_H = 8
WIN_W = 16
N_GROUPS = 8
EXPERTS_PER_GROUP = 8
N_EXPERTS = N_GROUPS * EXPERTS_PER_GROUP
TOP_K_INNER = 2
EXPERT_FF = D_MODEL // 4
DISPATCH_BLOCK = 256
RMS_EPS = 1e-6
LN_EPS = 1e-5
NEG_INF = -1e30

kernel_name = 'hybrid_conformer_natten_hmoe_encoder'


def rms_norm(x, g):
    xf = x.astype(jnp.float32)
    y = xf * lax.rsqrt(jnp.mean(xf * xf, axis=-1, keepdims=True) + RMS_EPS) * g.astype(jnp.float32)
    return y.astype(x.dtype)


def layer_norm(x, g, b):
    xf = x.astype(jnp.float32)
    mu = jnp.mean(xf, axis=-1, keepdims=True)
    var = jnp.mean(jnp.square(xf - mu), axis=-1, keepdims=True)
    y = (xf - mu) * lax.rsqrt(var + LN_EPS) * g.astype(jnp.float32) + b.astype(jnp.float32)
    return y.astype(x.dtype)


def conformer_conv(h, w_pw1, b_pw1, w_dw, b_dw, ln_g, ln_b, w_pw2, b_pw2):
    d = h.shape[-1]
    u = jax.nn.glu(h @ w_pw1 + b_pw1, axis=-1)
    u = lax.conv_general_dilated(u, w_dw[:, None, :], (1,),
                                 [(CONV_WIDTH // 2, CONV_WIDTH // 2)],
                                 dimension_numbers=('NWC', 'WIO', 'NWC'),
                                 feature_group_count=d) + b_dw
    u = jax.nn.silu(layer_norm(u, ln_g, ln_b))
    return u @ w_pw2 + b_pw2


def neighborhood_attention(h, w_qkv, rpb, w_o):
    b, s, d = h.shape
    rows = s // GRID_W
    kh = min(WIN_H, rows)
    qkv = (h @ w_qkv).reshape(b, rows, GRID_W, 3, NA_HEADS, NA_HEAD_DIM)
    q, k, v = qkv[:, :, :, 0], qkv[:, :, :, 1], qkv[:, :, :, 2]
    qc = np.arange(GRID_W)
    cs = np.clip(qc - WIN_W // 2, 0, GRID_W - WIN_W)
    kc = np.arange(GRID_W)
    col_mask = jnp.asarray((kc[None, :] >= cs[:, None]) & (kc[None, :] < cs[:, None] + WIN_W))
    col_idx = jnp.asarray(np.clip(kc[None, :] - qc[:, None] + WIN_W - 1, 0, 2 * WIN_W - 2))
    rpb_f = rpb.astype(jnp.float32)
    scale = NA_HEAD_DIM ** -0.5

    def row_block(r):
        rs = jnp.clip(r - kh // 2, 0, rows - kh)
        k_band = lax.dynamic_slice_in_dim(k, rs, kh, axis=1)
        v_band = lax.dynamic_slice_in_dim(v, rs, kh, axis=1)
        q_row = lax.dynamic_index_in_dim(q, r, axis=1, keepdims=False)
        sc = jnp.einsum('bqhd,brkhd->bhqrk', q_row, k_band).astype(jnp.float32) * scale
        row_idx = rs + jnp.arange(kh) - r + WIN_H - 1
        bias = rpb_f[:, row_idx[None, :, None], col_idx[:, None, :]]
        sc = jnp.where(col_mask[None, None, :, None, :], sc + bias[None], NEG_INF)
        p = jax.nn.softmax(sc.reshape(b, NA_HEADS, GRID_W, kh * GRID_W), axis=-1)
        p = p.reshape(b, NA_HEADS, GRID_W, kh, GRID_W).astype(v.dtype)
        return jnp.einsum('bhqrk,brkhd->bqhd', p, v_band)

    o = lax.map(row_block, jnp.arange(rows))
    o = jnp.transpose(o, (1, 0, 2, 3, 4)).reshape(b, s, d)
    return o @ w_o


def hier_moe(h, w_grp, b_grp, w_sub, b_sub, w_gate, w_up, w_down):
    n, d = h.shape
    grp_logits = jnp.einsum('nd,dg->ng', h, w_grp).astype(jnp.float32) + b_grp.astype(jnp.float32)
    grp_prob = jax.nn.softmax(grp_logits, axis=-1)
    p_top, g_top = lax.top_k(grp_prob, 1)
    sub_all = jnp.einsum('nd,gde->nge', h, w_sub).astype(jnp.float32) + b_sub.astype(jnp.float32)
    sub = jnp.take_along_axis(sub_all, g_top[:, :, None], axis=1)[:, 0]
    v_top, e_top = lax.top_k(sub, TOP_K_INNER)
    gate = p_top * jax.nn.softmax(v_top, axis=-1)
    expert = g_top * EXPERTS_PER_GROUP + e_top
    a = n * TOP_K_INNER
    e_flat = expert.reshape(a).astype(jnp.int32)
    gate_flat = gate.reshape(a)
    tok_flat = jnp.repeat(jnp.arange(n, dtype=jnp.int32), TOP_K_INNER)
    order = jnp.argsort(e_flat)
    e_sorted, tok_sorted, gate_sorted = e_flat[order], tok_flat[order], gate_flat[order]
    counts = jnp.bincount(e_flat, length=N_EXPERTS)
    padded = (counts + DISPATCH_BLOCK - 1) // DISPATCH_BLOCK * DISPATCH_BLOCK
    pad_end = jnp.cumsum(padded)
    pad_start = pad_end - padded
    seg_start = jnp.cumsum(counts) - counts
    dest = pad_start[e_sorted] + jnp.arange(a, dtype=jnp.int32) - seg_start[e_sorted]
    length = -(-a // DISPATCH_BLOCK) * DISPATCH_BLOCK + N_EXPERTS * DISPATCH_BLOCK
    n_blocks = length // DISPATCH_BLOCK
    buf_tok = jnp.zeros((length,), jnp.int32).at[dest].set(tok_sorted)
    buf_gate = jnp.zeros((length,), jnp.float32).at[dest].set(gate_sorted)
    blk_expert = jnp.minimum(jnp.searchsorted(pad_end, jnp.arange(n_blocks) * DISPATCH_BLOCK, side='right'),
                             N_EXPERTS - 1).astype(jnp.int32)

    def expert_block(args):
        tok, gt, e = args
        xb = h[tok]
        hb = jax.nn.silu(xb @ w_gate[e]) * (xb @ w_up[e])
        return (hb @ w_down[e]) * gt[:, None].astype(h.dtype)

    out = lax.map(expert_block, (buf_tok.reshape(n_blocks, DISPATCH_BLOCK),
                                 buf_gate.reshape(n_blocks, DISPATCH_BLOCK), blk_expert))
    return jnp.zeros_like(h).at[buf_tok].add(out.reshape(length, d))


def encoder_trunk(x, mix_norm, ffn_norm, final_norm, conv_p, na_p, moe_p):
    b, s, d = x.shape
    for i in range(DEPTH):
        j = i // N_MIXERS
        h = rms_norm(x, mix_norm[i])
        if i % N_MIXERS == 0:
            x = x + conformer_conv(h, *[p[j] for p in conv_p])
        else:
            x = x + neighborhood_attention(h, *[p[j] for p in na_p])
        h = rms_norm(x, ffn_norm[i])
        x = x + hier_moe(h.reshape(b * s, d), *[p[i] for p in moe_p]).reshape(b, s, d)
    return rms_norm(x, final_norm)


def setup_inputs(seed: int = 0) -> dict:
    key = jax.random.key(seed)
    ks = jax.random.split(key, 26)
    D = D_MODEL
    f32 = jnp.float32

    def nrm(k, shape, scale):
        return jax.random.normal(k, shape, f32) * scale

    return {
        'x_prompt': nrm(ks[0], (BATCH, SEQ, D), 1.0),
        'x_sample': nrm(ks[1], (DEC_BATCH, DEC_SEQ, D), 1.0),
        'mix_norm': 1.0 + nrm(ks[2], (DEPTH, D), 0.01),
        'ffn_norm': 1.0 + nrm(ks[3], (DEPTH, D), 0.01),
        'final_norm': 1.0 + nrm(ks[4], (D,), 0.01),
        'cv_w_pw1': nrm(ks[5], (N_CONV_LAYERS, D, 2 * D), D ** -0.5),
        'cv_b_pw1': nrm(ks[6], (N_CONV_LAYERS, 2 * D), 0.01),
        'cv_w_dw': nrm(ks[7], (N_CONV_LAYERS, CONV_WIDTH, D), CONV_WIDTH ** -0.5),
        'cv_b_dw': nrm(ks[8], (N_CONV_LAYERS, D), 0.01),
        'cv_ln_g': 1.0 + nrm(ks[9], (N_CONV_LAYERS, D), 0.01),
        'cv_ln_b': nrm(ks[10], (N_CONV_LAYERS, D), 0.01),
        'cv_w_pw2': nrm(ks[11], (N_CONV_LAYERS, D, D), D ** -0.5),
        'cv_b_pw2': nrm(ks[12], (N_CONV_LAYERS, D), 0.01),
        'na_w_qkv': nrm(ks[13], (N_ATTN_LAYERS, D, 3 * D), D ** -0.5),
        'na_rpb': nrm(ks[14], (N_ATTN_LAYERS, NA_HEADS, 2 * WIN_H - 1, 2 * WIN_W - 1), 0.02),
        'na_w_o': nrm(ks[15], (N_ATTN_LAYERS, D, D), D ** -0.5),
        'moe_w_grp': nrm(ks[16], (DEPTH, D, N_GROUPS), D ** -0.5),
        'moe_b_grp': nrm(ks[17], (DEPTH, N_GROUPS), 0.01),
        'moe_w_sub': nrm(ks[18], (DEPTH, N_GROUPS, D, EXPERTS_PER_GROUP), D ** -0.5),
        'moe_b_sub': nrm(ks[19], (DEPTH, N_GROUPS, EXPERTS_PER_GROUP), 0.01),
        'moe_w_gate': nrm(ks[20], (DEPTH, N_EXPERTS, D, EXPERT_FF), D ** -0.5),
        'moe_w_up': nrm(ks[21], (DEPTH, N_EXPERTS, D, EXPERT_FF), D ** -0.5),
        'moe_w_down': nrm(ks[22], (DEPTH, N_EXPERTS, EXPERT_FF, D), EXPERT_FF ** -0.5),
    }


def reference(x_prompt, x_sample, mix_norm, ffn_norm, final_norm,
              cv_w_pw1, cv_b_pw1, cv_w_dw, cv_b_dw, cv_ln_g, cv_ln_b, cv_w_pw2, cv_b_pw2,
              na_w_qkv, na_rpb, na_w_o,
              moe_w_grp, moe_b_grp, moe_w_sub, moe_b_sub, moe_w_gate, moe_w_up, moe_w_down):
    conv_p = (cv_w_pw1, cv_b_pw1, cv_w_dw, cv_b_dw, cv_ln_g, cv_ln_b, cv_w_pw2, cv_b_pw2)
    na_p = (na_w_qkv, na_rpb, na_w_o)
    moe_p = (moe_w_grp, moe_b_grp, moe_w_sub, moe_b_sub, moe_w_gate, moe_w_up, moe_w_down)
    y_prompt = encoder_trunk(x_prompt, mix_norm, ffn_norm, final_norm, conv_p, na_p, moe_p)
    y_sample = encoder_trunk(x_sample, mix_norm, ffn_norm, final_norm, conv_p, na_p, moe_p)
    return (y_prompt, y_sample)
```

```python
import functools

import numpy as np
import jax
import jax.numpy as jnp
from jax import lax
from jax.experimental import pallas as pl
from jax.experimental.pallas import tpu as pltpu

GRID_W = 64
CONV_WIDTH = 31
NA_HEAD_DIM = 128
WIN_H = 8
WIN_W = 16
N_GROUPS = 8
EXPERTS_PER_GROUP = 8
N_EXPERTS = N_GROUPS * EXPERTS_PER_GROUP
TOP_K_INNER = 2
RMS_EPS = 1e-6
LN_EPS = 1e-5
NEG_INF = -1e30

LANES = 128
BF16_SUBLANES = 16
V7X_VMEM_BUDGET = 58 * 1024 * 1024

MM_TM = 1024
MM_TN = 512
NORM_TM = 256
CONV_TS = 256
CONV_HALO = BF16_SUBLANES
CONV_RB = 64
LN_RB = 32
NA_QROWS = 4
NA_KROWS = NA_QROWS + WIN_H
ROUTER_TM = 256
ROUTE_LANES = LANES
EXPERT_ROWS = 1024
EXPERT_TF = 256
EXPERT_TN = 512
COMBINE_TC = 128

F32 = jnp.float32
BF16 = jnp.bfloat16


def _tile(n, pref):
    t = min(n, pref)
    assert n % t == 0, (n, pref)
    return t


def _params(*sem):
    return pltpu.CompilerParams(dimension_semantics=sem, vmem_limit_bytes=V7X_VMEM_BUDGET)


def _rmsnorm_kernel(x_ref, g_ref, o_ref):
    x = x_ref[...]
    ms = jnp.mean(x * x, axis=-1, keepdims=True)
    o_ref[...] = (x * lax.rsqrt(ms + RMS_EPS) * g_ref[...]).astype(o_ref.dtype)


def _rmsnorm(x, g, out_dtype):
    n, d = x.shape
    tm = _tile(n, NORM_TM)
    return pl.pallas_call(
        _rmsnorm_kernel,
        grid=(n // tm,),
        in_specs=[pl.BlockSpec((tm, d), lambda i: (i, 0)),
                  pl.BlockSpec((1, d), lambda i: (0, 0))],
        out_specs=pl.BlockSpec((tm, d), lambda i: (i, 0)),
        out_shape=jax.ShapeDtypeStruct((n, d), out_dtype),
        compiler_params=_params("parallel"),
        name="rmsnorm",
    )(x, g.reshape(1, d))


def _glu_kernel(h_ref, wa_ref, wg_ref, ba_ref, bg_ref, o_ref):
    h = h_ref[...]
    a = jnp.dot(h, wa_ref[...], preferred_element_type=F32) + ba_ref[...]
    g = jnp.dot(h, wg_ref[...], preferred_element_type=F32) + bg_ref[...]
    o_ref[...] = (a * jax.nn.sigmoid(g)).astype(o_ref.dtype)


def _glu_matmul(h, w, b):
    n, k = h.shape
    d = w.shape[1] // 2
    tm, tn = _tile(n, MM_TM), _tile(d, MM_TN)
    nj = d // tn
    b2 = b.reshape(1, 2 * d)
    return pl.pallas_call(
        _glu_kernel,
        grid=(n // tm, nj),
        in_specs=[pl.BlockSpec((tm, k), lambda i, j: (i, 0)),
                  pl.BlockSpec((k, tn), lambda i, j: (0, j)),
                  pl.BlockSpec((k, tn), lambda i, j: (0, j + nj)),
                  pl.BlockSpec((1, tn), lambda i, j: (0, j)),
                  pl.BlockSpec((1, tn), lambda i, j: (0, j + nj))],
        out_specs=pl.BlockSpec((tm, tn), lambda i, j: (i, j)),
        out_shape=jax.ShapeDtypeStruct((n, d), BF16),
        compiler_params=_params("parallel", "arbitrary"),
        name="pw1_glu",
    )(h, w, w, b2, b2)


def _mm_kernel(a_ref, w_ref, o_ref):
    o_ref[...] = jnp.dot(a_ref[...], w_ref[...], preferred_element_type=F32).astype(o_ref.dtype)


def _mm_bias_res_kernel(a_ref, w_ref, b_ref, r_ref, o_ref):
    o_ref[...] = r_ref[...] + (jnp.dot(a_ref[...], w_ref[...], preferred_element_type=F32) + b_ref[...])


def _mm_res_kernel(a_ref, w_ref, r_ref, o_ref):
    o_ref[...] = r_ref[...] + jnp.dot(a_ref[...], w_ref[...], preferred_element_type=F32)


def _matmul(a, w, *, bias=None, res=None, out_dtype=F32, name="matmul"):
    n, k = a.shape
    m = w.shape[1]
    tm, tn = _tile(n, MM_TM), _tile(m, MM_TN)
    a_spec = pl.BlockSpec((tm, k), lambda i, j: (i, 0))
    w_spec = pl.BlockSpec((k, tn), lambda i, j: (0, j))
    o_spec = pl.BlockSpec((tm, tn), lambda i, j: (i, j))
    if res is None:
        assert bias is None
        kern, specs, args = _mm_kernel, [a_spec, w_spec], (a, w)
    elif bias is None:
        kern, specs, args = _mm_res_kernel, [a_spec, w_spec, o_spec], (a, w, res)
    else:
        kern = _mm_bias_res_kernel
        specs = [a_spec, w_spec, pl.BlockSpec((1, tn), lambda i, j: (0, j)), o_spec]
        args = (a, w, bias.reshape(1, m), res)
    return pl.pallas_call(
        kern,
        grid=(n // tm, m // tn),
        in_specs=specs,
        out_specs=o_spec,
        out_shape=jax.ShapeDtypeStruct((n, m), out_dtype),
        compiler_params=_params("parallel", "arbitrary"),
        name=name,
    )(*args)


def _conv_kernel(prev_ref, cur_ref, next_ref, w_ref, bdw_ref, g_ref, b_ref, o_ref,
                 win_ref, acc_ref, *, ts, n_tblk, d):
    i = pl.program_id(1)
    halo = CONV_HALO
    n_cb = d // LANES
    has_prev = i > 0
    has_next = i < n_tblk - 1
    for cb in range(n_cb):
        sl = slice(cb * LANES, (cb + 1) * LANES)
        win_ref[cb, 0:halo, :] = jnp.where(has_prev, prev_ref[:, sl].astype(F32), 0.0)
        win_ref[cb, halo:halo + ts, :] = cur_ref[:, sl].astype(F32)
        win_ref[cb, halo + ts:halo + ts + halo, :] = jnp.where(has_next, next_ref[:, sl].astype(F32), 0.0)

    base = halo - CONV_WIDTH // 2
    span = 2 * CONV_RB
    starts = [sp * span + par for sp in range(ts // span) for par in (0, 1)]

    def col_body(cb, carry):
        accs = [jnp.zeros((CONV_RB, LANES), F32) for _ in starts]
        for k in range(CONV_WIDTH):
            wk = w_ref[cb, k:k + 1, :]
            for j, st in enumerate(starts):
                accs[j] = accs[j] + win_ref[cb, pl.ds(st + base + k, CONV_RB, stride=2), :] * wk
        bias = bdw_ref[cb]
        for j, st in enumerate(starts):
            acc_ref[cb, pl.ds(st, CONV_RB, stride=2), :] = accs[j] + bias
        return carry

    lax.fori_loop(0, n_cb, col_body, 0)

    inv_d = 1.0 / d

    def ln_body(rb, carry):
        r0 = pl.multiple_of(rb * LN_RB, LN_RB)
        y = acc_ref[:, pl.ds(r0, LN_RB), :]
        mu = jnp.sum(jnp.sum(y, axis=0), axis=-1, keepdims=True) * inv_d
        yc = y - mu[None]
        var = jnp.sum(jnp.sum(yc * yc, axis=0), axis=-1, keepdims=True) * inv_d
        z = yc * lax.rsqrt(var + LN_EPS)[None] * g_ref[...] + b_ref[...]
        z = (z * jax.nn.sigmoid(z)).astype(o_ref.dtype)
        for cb in range(n_cb):
            o_ref[pl.ds(r0, LN_RB), cb * LANES:(cb + 1) * LANES] = z[cb]
        return carry

    lax.fori_loop(0, ts // LN_RB, ln_body, 0)


def _conv_ln_swish(u, w_dw, b_dw, ln_g, ln_b, seq):
    n, d = u.shape
    n_seq = n // seq
    ts = _tile(seq, CONV_TS)
    n_tblk = seq // ts
    halo = CONV_HALO
    n_cb = d // LANES
    assert halo >= CONV_WIDTH // 2 and ts % halo == 0 and ts % (2 * CONV_RB) == 0 and d % LANES == 0
    hb = ts // halo
    last_hblk = n // halo - 1

    def prev_map(b, i):
        return (jnp.maximum((b * n_tblk + i) * hb - 1, 0), 0)

    def next_map(b, i):
        return (jnp.minimum((b * n_tblk + i + 1) * hb, last_hblk), 0)

    slab = lambda v: v.reshape(-1, n_cb, LANES).transpose(1, 0, 2)
    vec = lambda b, i: (0, 0, 0)
    kern = functools.partial(_conv_kernel, ts=ts, n_tblk=n_tblk, d=d)
    return pl.pallas_call(
        kern,
        grid=(n_seq, n_tblk),
        in_specs=[pl.BlockSpec((halo, d), prev_map),
                  pl.BlockSpec((ts, d), lambda b, i: (b * n_tblk + i, 0)),
                  pl.BlockSpec((halo, d), next_map),
                  pl.BlockSpec((n_cb, CONV_WIDTH, LANES), vec),
                  pl.BlockSpec((n_cb, 1, LANES), vec),
                  pl.BlockSpec((n_cb, 1, LANES), vec),
                  pl.BlockSpec((n_cb, 1, LANES), vec)],
        out_specs=pl.BlockSpec((ts, d), lambda b, i: (b * n_tblk + i, 0)),
        out_shape=jax.ShapeDtypeStruct((n, d), BF16),
        scratch_shapes=[pltpu.VMEM((n_cb, ts + 2 * halo, LANES), F32),
                        pltpu.VMEM((n_cb, ts, LANES), F32)],
        compiler_params=_params("parallel", "arbitrary"),
        name="dwconv_ln_swish",
    )(u, u, u, slab(w_dw), slab(b_dw), slab(ln_g), slab(ln_b))


def _natten_plan(rows):
    kh = min(WIN_H, rows)
    assert rows % NA_QROWS == 0 and rows >= NA_KROWS
    n_groups = rows // NA_QROWS
    k_start = np.clip(np.arange(n_groups) * NA_QROWS - kh // 2, 0, rows - NA_KROWS)
    patterns, slab_of = [], []
    for g in range(n_groups):
        pat = np.full((NA_QROWS, NA_KROWS), -1, np.int64)
        for qi in range(NA_QROWS):
            r = g * NA_QROWS + qi
            rs = int(np.clip(r - kh // 2, 0, rows - kh))
            for j in range(NA_KROWS):
                kr = int(k_start[g]) + j
                if rs <= kr < rs + kh:
                    pat[qi, j] = kr - r + WIN_H - 1
            assert (pat[qi] >= 0).sum() == kh
        key = pat.tobytes()
        if key not in [p.tobytes() for p in patterns]:
            patterns.append(pat)
        slab_of.append([p.tobytes() for p in patterns].index(key))
    return k_start.astype(np.int32), np.asarray(slab_of, np.int32), np.stack(patterns)


def _natten_bias(rpb, patterns):
    w = GRID_W
    qc = np.arange(w)
    cs = np.clip(qc - WIN_W // 2, 0, w - WIN_W)
    kc = np.arange(w)
    col_mask = (kc[None, :] >= cs[:, None]) & (kc[None, :] < cs[:, None] + WIN_W)
    col_idx = np.clip(kc[None, :] - qc[:, None] + WIN_W - 1, 0, 2 * WIN_W - 2)
    n_slabs = patterns.shape[0]
    ridx = np.maximum(patterns, 0)[:, :, None, :, None]
    cidx = col_idx[None, None, :, None, :]
    valid = (patterns >= 0)[:, :, None, :, None] & col_mask[None, None, :, None, :]
    ridx = np.broadcast_to(ridx, valid.shape)
    cidx = np.broadcast_to(cidx, valid.shape)
    tab = rpb.astype(F32)[:, ridx, cidx]
    tab = jnp.where(jnp.asarray(valid)[None], tab, NEG_INF)
    return tab.reshape(rpb.shape[0], n_slabs, NA_QROWS * w, NA_KROWS * w)


def _natten_kernel(kstart_ref, slab_ref, q_ref, k_ref, v_ref, bias_ref, o_ref, *, n_groups, scale):
    gq = NA_QROWS * GRID_W
    gk = NA_KROWS * GRID_W

    def body(g, carry):
        q0 = pl.multiple_of(g * gq, gq)
        k0 = pl.multiple_of(kstart_ref[g] * GRID_W, GRID_W)
        q = q_ref[pl.ds(q0, gq), :]
        kb = k_ref[pl.ds(k0, gk), :]
        vb = v_ref[pl.ds(k0, gk), :]
        s = lax.dot_general(q, kb, (((1,), (1,)), ((), ())), preferred_element_type=F32) * scale
        s = s + bias_ref[0, slab_ref[g]]
        m = jnp.max(s, axis=-1, keepdims=True)
        e = jnp.exp(s - m)
        p = (e / jnp.sum(e, axis=-1, keepdims=True)).astype(BF16)
        o_ref[pl.ds(q0, gq), :] = jnp.dot(p, vb, preferred_element_type=F32).astype(o_ref.dtype)
        return carry

    lax.fori_loop(0, n_groups, body, 0)


def _natten(qkv, rpb, seq):
    n, d3 = qkv.shape
    d = d3 // 3
    heads = d // NA_HEAD_DIM
    n_seq = n // seq
    rows = seq // GRID_W
    k_start, slab_of, patterns = _natten_plan(rows)
    bias = _natten_bias(rpb, patterns)
    n_slabs = patterns.shape[0]
    gq, gk = NA_QROWS * GRID_W, NA_KROWS * GRID_W
    kern = functools.partial(_natten_kernel, n_groups=rows // NA_QROWS, scale=NA_HEAD_DIM ** -0.5)
    grid_spec = pltpu.PrefetchScalarGridSpec(
        num_scalar_prefetch=2,
        grid=(heads, n_seq),
        in_specs=[pl.BlockSpec((seq, NA_HEAD_DIM), lambda h, b, ks, sl: (b, h)),
                  pl.BlockSpec((seq, NA_HEAD_DIM), lambda h, b, ks, sl: (b, heads + h)),
                  pl.BlockSpec((seq, NA_HEAD_DIM), lambda h, b, ks, sl: (b, 2 * heads + h)),
                  pl.BlockSpec((1, n_slabs, gq, gk), lambda h, b, ks, sl: (h, 0, 0, 0))],
        out_specs=pl.BlockSpec((seq, NA_HEAD_DIM), lambda h, b, ks, sl: (b, h)),
    )
    return pl.pallas_call(
        kern,
        grid_spec=grid_spec,
        out_shape=jax.ShapeDtypeStruct((n, d), BF16),
        compiler_params=_params("parallel", "arbitrary"),
        name="natten",
    )(jnp.asarray(k_start), jnp.asarray(slab_of), qkv, qkv, qkv, bias)


def _router_kernel(x_ref, g_ref, whi_ref, wlo_ref, b_ref, hp_ref, route_ref, *, d):
    x = x_ref[...]
    ms = jnp.mean(x * x, axis=-1, keepdims=True)
    hn = x * lax.rsqrt(ms + RMS_EPS) * g_ref[...]
    hb = hn.astype(BF16)
    hbf = hb.astype(F32)
    hlo = (hn - hbf).astype(BF16)
    logits = (jnp.dot(hb, whi_ref[...], preferred_element_type=F32)
              + (jnp.dot(hb, wlo_ref[...], preferred_element_type=F32)
                 + jnp.dot(hlo, whi_ref[...], preferred_element_type=F32))) + b_ref[...]

    lane = lax.broadcasted_iota(jnp.int32, logits.shape, 1)
    big = jnp.int32(ROUTE_LANES)
    is_grp = lane < N_GROUPS
    gl = jnp.where(is_grp, logits, NEG_INF)
    gmax = jnp.max(gl, axis=-1, keepdims=True)
    gsum = jnp.sum(jnp.where(is_grp, jnp.exp(gl - gmax), 0.0), axis=-1, keepdims=True)
    p_top = 1.0 / gsum
    g_top = jnp.min(jnp.where(is_grp & (gl == gmax), lane, big), axis=-1, keepdims=True)
    lo = N_GROUPS + g_top * EXPERTS_PER_GROUP
    in_grp = (lane >= lo) & (lane < lo + EXPERTS_PER_GROUP)
    sl = jnp.where(in_grp, logits, NEG_INF)
    v1 = jnp.max(sl, axis=-1, keepdims=True)
    i1 = jnp.min(jnp.where(in_grp & (sl == v1), lane, big), axis=-1, keepdims=True)
    rest = in_grp & (lane != i1)
    sl2 = jnp.where(rest, logits, NEG_INF)
    v2 = jnp.max(sl2, axis=-1, keepdims=True)
    i2 = jnp.min(jnp.where(rest & (sl2 == v2), lane, big), axis=-1, keepdims=True)
    ex = jnp.exp(v2 - v1)
    den = 1.0 + ex
    gate1 = p_top * (1.0 / den)
    gate2 = p_top * (ex / den)
    e1 = (i1 - N_GROUPS).astype(F32)
    e2 = (i2 - N_GROUPS).astype(F32)
    route_ref[...] = jnp.where(lane == 0, e1,
                               jnp.where(lane == 1, e2,
                                         jnp.where(lane == 2, gate1,
                                                   jnp.where(lane == 3, gate2, 0.0))))

    bits = lax.bitcast_convert_type(hbf, jnp.uint32)
    half = d // 2
    hp_ref[...] = bits[:, :half] | (bits[:, half:] >> 16)


def _router(x, g, w_hi, w_lo, b):
    n, d = x.shape
    tm = _tile(n, ROUTER_TM)
    kern = functools.partial(_router_kernel, d=d)
    vec = lambda i: (0, 0)
    return pl.pallas_call(
        kern,
        grid=(n // tm,),
        in_specs=[pl.BlockSpec((tm, d), lambda i: (i, 0)),
                  pl.BlockSpec((1, d), vec),
                  pl.BlockSpec((d, ROUTE_LANES), vec),
                  pl.BlockSpec((d, ROUTE_LANES), vec),
                  pl.BlockSpec((1, ROUTE_LANES), vec)],
        out_specs=[pl.BlockSpec((tm, d // 2), lambda i: (i, 0)),
                   pl.BlockSpec((tm, ROUTE_LANES), lambda i: (i, 0))],
        out_shape=[jax.ShapeDtypeStruct((n, d // 2), jnp.uint32),
                   jax.ShapeDtypeStruct((n, ROUTE_LANES), F32)],
        compiler_params=_params("parallel"),
        name="moe_router",
    )(x, g.reshape(1, d), w_hi, w_lo, b)


def _row_copy(src_ref, src_row, dst_ref, dst_row, sem):
    return pltpu.make_async_copy(src_ref.at[pl.ds(src_row, 1)], dst_ref.at[pl.ds(dst_row, 1)], sem)


def _expert_kernel(ce_ref, nu_ref, tok_ref, hp_ref, wg_ref, wu_ref, wd_ref, y_ref,
                   xraw_ref, xb_ref, h_ref, sem, *, rows, n_f, tf, rows_per_step):
    c = pl.program_id(0)
    s = pl.program_id(1)
    n_used = nu_ref[0]
    computing = (c >= 1) & (c - 1 < n_used)

    @pl.when(computing & (s == 0))
    def _():
        pltpu.make_async_copy(hp_ref.at[pl.ds(0, rows)], xraw_ref, sem).wait()
        w = xraw_ref[...]
        half = w.shape[1]
        xb_ref[:, :half] = lax.bitcast_convert_type(w & jnp.uint32(0xFFFF0000), F32).astype(BF16)
        xb_ref[:, half:] = lax.bitcast_convert_type(w << 16, F32).astype(BF16)

    @pl.when((c < n_used) & (s < n_f))
    def _():
        for i in range(rows_per_step):
            r = s * rows_per_step + i
            _row_copy(hp_ref, tok_ref[0, 0, r], xraw_ref, r, sem).start()

    @pl.when(computing & (s < n_f))
    def _():
        xb = xb_ref[...]
        g = jnp.dot(xb, wg_ref[0, 0].astype(BF16), preferred_element_type=F32)
        u = jnp.dot(xb, wu_ref[0, 0].astype(BF16), preferred_element_type=F32)
        col = pl.multiple_of(s * tf, tf)
        h_ref[:, pl.ds(col, tf)] = (g * jax.nn.sigmoid(g) * u).astype(BF16)

    @pl.when(computing & (s >= n_f))
    def _():
        y_ref[...] = jnp.dot(h_ref[...], wd_ref[0, 0].astype(BF16), preferred_element_type=F32)


def _experts(hp, tok_chunks, chunk_expert, n_used, w_gate, w_up, w_down, layer):
    n, half = hp.shape
    d = 2 * half
    ff = w_gate.shape[-1]
    n_max, rows = tok_chunks.shape
    tf, tn = _tile(ff, EXPERT_TF), _tile(d, EXPERT_TN)
    n_f, n_n = ff // tf, d // tn
    steps = n_f + n_n
    assert rows % n_f == 0
    rows_per_step = rows // n_f

    def chunk(c, nu):
        return jnp.clip(c - 1, 0, nu[0] - 1)

    def active(c, nu):
        return (c >= 1) & (c - 1 < nu[0])

    def gate_map(c, s, ce, nu):
        nxt = jnp.clip(c, 0, nu[0] - 1)
        in_p1 = active(c, nu) & (s < n_f)
        has_next = c < nu[0]
        e = jnp.where(in_p1, ce[chunk(c, nu)], jnp.where(has_next, ce[nxt], ce[chunk(c, nu)]))
        t = jnp.where(in_p1, s, jnp.where(has_next, 0, n_f - 1))
        return (layer, e, 0, t)

    def down_map(c, s, ce, nu):
        t = jnp.where(active(c, nu), jnp.maximum(s - n_f, 0), jnp.where(c == 0, 0, n_n - 1))
        return (layer, ce[chunk(c, nu)], 0, t)

    def out_map(c, s, ce, nu):
        t = jnp.where(active(c, nu), jnp.maximum(s - n_f, 0), jnp.where(c == 0, 0, n_n - 1))
        return (chunk(c, nu), t)

    kern = functools.partial(_expert_kernel, rows=rows, n_f=n_f, tf=tf, rows_per_step=rows_per_step)
    grid_spec = pltpu.PrefetchScalarGridSpec(
        num_scalar_prefetch=2,
        grid=(n_max + 1, steps),
        in_specs=[pl.BlockSpec((1, 1, rows), lambda c, s, ce, nu: (jnp.minimum(c, n_max - 1), 0, 0),
                               memory_space=pltpu.SMEM),
                  pl.BlockSpec(memory_space=pl.ANY),
                  pl.BlockSpec((1, 1, d, tf), gate_map),
                  pl.BlockSpec((1, 1, d, tf), gate_map),
                  pl.BlockSpec((1, 1, ff, tn), down_map)],
        out_specs=pl.BlockSpec((rows, tn), out_map),
        scratch_shapes=[pltpu.VMEM((rows, half), jnp.uint32),
                        pltpu.VMEM((rows, d), BF16),
                        pltpu.VMEM((rows, ff), BF16),
                        pltpu.SemaphoreType.DMA(())],
    )
    return pl.pallas_call(
        kern,
        grid_spec=grid_spec,
        out_shape=jax.ShapeDtypeStruct((n_max * rows, d), F32),
        compiler_params=_params("arbitrary", "arbitrary"),
        name="moe_experts",
    )(chunk_expert, n_used.reshape(1), tok_chunks.reshape(n_max, 1, rows), hp, w_gate, w_up, w_down)


def _combine_kernel(pos_ref, x_ref, route_ref, gn_ref, y_ref, *rest, tc, emit_x):
    if emit_x:
        xo_ref, ho_ref, ybuf_ref, sem = rest
    else:
        ho_ref, ybuf_ref, sem = rest

    def issue(a, carry):
        dst = (a & 1) * tc + (a >> 1)
        _row_copy(y_ref, pos_ref[0, 0, a], ybuf_ref, dst, sem).start()
        return carry

    lax.fori_loop(0, 2 * tc, issue, 0)
    pltpu.make_async_copy(y_ref.at[pl.ds(0, 2 * tc)], ybuf_ref, sem).wait()

    route = route_ref[...]
    x = x_ref[...] + (route[:, 2:3] * ybuf_ref[0:tc, :] + route[:, 3:4] * ybuf_ref[tc:2 * tc, :])
    if emit_x:
        xo_ref[...] = x
    ms = jnp.mean(x * x, axis=-1, keepdims=True)
    ho_ref[...] = (x * lax.rsqrt(ms + RMS_EPS) * gn_ref[...]).astype(ho_ref.dtype)


def _combine(x, y, pos, route, g_next, *, emit_x, h_dtype):
    n, d = x.shape
    tc = _tile(n, COMBINE_TC)
    nb = n // tc
    kern = functools.partial(_combine_kernel, tc=tc, emit_x=emit_x)
    row = pl.BlockSpec((tc, d), lambda i: (i, 0))
    out_specs = [row, row] if emit_x else [row]
    out_shape = ([jax.ShapeDtypeStruct((n, d), F32)] if emit_x else []) + [jax.ShapeDtypeStruct((n, d), h_dtype)]
    return pl.pallas_call(
        kern,
        grid=(nb,),
        in_specs=[pl.BlockSpec((1, 1, 2 * tc), lambda i: (i, 0, 0), memory_space=pltpu.SMEM),
                  row,
                  pl.BlockSpec((tc, ROUTE_LANES), lambda i: (i, 0)),
                  pl.BlockSpec((1, d), lambda i: (0, 0)),
                  pl.BlockSpec(memory_space=pl.ANY)],
        out_specs=out_specs,
        out_shape=out_shape,
        scratch_shapes=[pltpu.VMEM((2 * tc, d), F32), pltpu.SemaphoreType.DMA(())],
        compiler_params=_params("arbitrary"),
        name="moe_combine",
    )(pos.reshape(nb, 1, 2 * tc), x, route, g_next.reshape(1, d), y)


def _dispatch_tables(route, rows):
    n = route.shape[0]
    a = n * TOP_K_INNER
    n_max = a // rows + N_EXPERTS
    e_flat = route[:, :TOP_K_INNER].astype(jnp.int32).reshape(a)
    order = jnp.argsort(e_flat, stable=True).astype(jnp.int32)
    sorted_pos = jnp.argsort(order).astype(jnp.int32)
    counts = jnp.sum((e_flat[:, None] == jnp.arange(N_EXPERTS, dtype=jnp.int32)[None, :]).astype(jnp.int32), axis=0)
    seg_start = jnp.cumsum(counts) - counts
    n_chunks = (counts + rows - 1) // rows
    chunk_end = jnp.cumsum(n_chunks)
    chunk_base = chunk_end - n_chunks
    n_used = chunk_end[-1].astype(jnp.int32)
    cidx = jnp.arange(n_max, dtype=jnp.int32)
    ce = jnp.minimum(jnp.searchsorted(chunk_end, cidx, side='right'), N_EXPERTS - 1).astype(jnp.int32)
    ce = jnp.where(cidx < n_used, ce, ce[n_used - 1])
    ustart = seg_start[ce] + (cidx - chunk_base[ce]) * rows
    tok_sorted = jnp.concatenate([order // TOP_K_INNER, jnp.zeros((rows,), jnp.int32)])
    tok_chunks = jax.vmap(lambda s0: lax.dynamic_slice(tok_sorted, (s0,), (rows,)))(jnp.minimum(ustart, a))
    pos = chunk_base[e_flat] * rows + (sorted_pos - seg_start[e_flat])
    return tok_chunks.astype(jnp.int32), ce, n_used, pos.astype(jnp.int32)


def _moe(x, layer, ffn_norm, moe_w_grp, moe_b_grp, moe_w_sub, moe_b_sub, moe_w_gate, moe_w_up, moe_w_down,
         g_next, *, last):
    n, d = x.shape
    w_r = jnp.concatenate([moe_w_grp[layer],
                           jnp.transpose(moe_w_sub[layer], (1, 0, 2)).reshape(d, N_EXPERTS)], axis=1)
    w_r = jnp.pad(w_r, ((0, 0), (0, ROUTE_LANES - w_r.shape[1])))
    w_hi = w_r.astype(BF16)
    w_lo = (w_r - w_hi.astype(F32)).astype(BF16)
    b_r = jnp.concatenate([moe_b_grp[layer], moe_b_sub[layer].reshape(N_EXPERTS)])
    b_r = jnp.pad(b_r, (0, ROUTE_LANES - b_r.shape[0])).reshape(1, ROUTE_LANES).astype(F32)

    hp, route = _router(x, ffn_norm[layer], w_hi, w_lo, b_r)
    rows = _tile(n * TOP_K_INNER, EXPERT_ROWS)
    tok_chunks, ce, n_used, pos = _dispatch_tables(route, rows)
    y = _experts(hp, tok_chunks, ce, n_used, moe_w_gate, moe_w_up, moe_w_down, layer)
    if last:
        (out,) = _combine(x, y, pos, route, g_next, emit_x=False, h_dtype=F32)
        return out
    return _combine(x, y, pos, route, g_next, emit_x=True, h_dtype=BF16)


def kernel(x_prompt, x_sample, mix_norm, ffn_norm, final_norm, cv_w_pw1, cv_b_pw1, cv_w_dw, cv_b_dw, cv_ln_g, cv_ln_b, cv_w_pw2, cv_b_pw2, na_w_qkv, na_rpb, na_w_o, moe_w_grp, moe_b_grp, moe_w_sub, moe_b_sub, moe_w_gate, moe_w_up, moe_w_down):
    bp, seq, d = x_prompt.shape
    bs = x_sample.shape[0]
    assert x_sample.shape[1:] == (seq, d) and seq % GRID_W == 0
    n_p = bp * seq
    x = jnp.concatenate([x_prompt.reshape(n_p, d), x_sample.reshape(bs * seq, d)], axis=0)
    moe_p = (ffn_norm, moe_w_grp, moe_b_grp, moe_w_sub, moe_b_sub, moe_w_gate, moe_w_up, moe_w_down)

    h = _rmsnorm(x, mix_norm[0], BF16)
    u = _glu_matmul(h, cv_w_pw1[0].astype(BF16), cv_b_pw1[0])
    v = _conv_ln_swish(u, cv_w_dw[0], cv_b_dw[0], cv_ln_g[0], cv_ln_b[0], seq)
    x = _matmul(v, cv_w_pw2[0].astype(BF16), bias=cv_b_pw2[0], res=x, name="pw2_res")
    x, h = _moe(x, 0, *moe_p, mix_norm[1], last=False)

    qkv = _matmul(h, na_w_qkv[0].astype(BF16), out_dtype=BF16, name="qkv")
    o = _natten(qkv, na_rpb[0], seq)
    x = _matmul(o, na_w_o[0].astype(BF16), res=x, name="wo_res")
    y = _moe(x, 1, *moe_p, final_norm, last=True)
    return (y[:n_p].reshape(bp, seq, d), y[n_p:].reshape(bs, seq, d))
```

```python
import functools

import numpy as np
import jax
import jax.numpy as jnp
from jax import lax
from jax.experimental import pallas as pl
from jax.experimental.pallas import tpu as pltpu

GRID_W = 64
CONV_WIDTH = 31
NA_HEAD_DIM = 128
WIN_H = 8
WIN_W = 16
N_GROUPS = 8
EXPERTS_PER_GROUP = 8
N_EXPERTS = N_GROUPS * EXPERTS_PER_GROUP
TOP_K_INNER = 2
RMS_EPS = 1e-6
LN_EPS = 1e-5
NEG_INF = -1e30
LOG2_E = 1.4426950408889634

LANES = 128
BF16_SUBLANES = 16
V7X_VMEM_BUDGET = 58 * 1024 * 1024

MM_TM = 1024
MM_TN = 512
NORM_TM = 256
CONV_TS = 256
CONV_HALO = BF16_SUBLANES
CONV_RB = 64
LN_RB = 32
NA_QROWS = 4
NA_KROWS = NA_QROWS + WIN_H
NA_UNROLL = 4
ROUTER_TM = 256
ROUTE_LANES = LANES
EXPERT_ROWS = 1024
EXPERT_SUB = 256
EXPERT_TF = 256
EXPERT_TN = 512
COMBINE_TC = 128

F32 = jnp.float32
BF16 = jnp.bfloat16


def _tile(n, pref):
    t = min(n, pref)
    assert n % t == 0, (n, pref)
    return t


def _params(*sem):
    return pltpu.CompilerParams(dimension_semantics=sem, vmem_limit_bytes=V7X_VMEM_BUDGET)


def _first_blocks(nb_first):
    return (lambda i: (jnp.minimum(i, nb_first - 1), 0)), (lambda i: (jnp.maximum(i - nb_first, 0), 0))


def _rmsnorm_kernel(xa_ref, xb_ref, g_ref, o_ref, *, nb_first):
    def norm(x_ref):
        x = x_ref[...]
        ms = jnp.mean(x * x, axis=-1, keepdims=True)
        o_ref[...] = (x * lax.rsqrt(ms + RMS_EPS) * g_ref[...]).astype(o_ref.dtype)

    i = pl.program_id(0)
    pl.when(i < nb_first)(lambda: norm(xa_ref))
    pl.when(i >= nb_first)(lambda: norm(xb_ref))


def _rmsnorm_stacked(xa, xb, g, out_dtype):
    na, d = xa.shape
    n = na + xb.shape[0]
    tm = _tile(na, NORM_TM)
    assert xb.shape[0] % tm == 0
    a_map, b_map = _first_blocks(na // tm)
    return pl.pallas_call(
        functools.partial(_rmsnorm_kernel, nb_first=na // tm),
        grid=(n // tm,),
        in_specs=[pl.BlockSpec((tm, d), a_map),
                  pl.BlockSpec((tm, d), b_map),
                  pl.BlockSpec((1, d), lambda i: (0, 0))],
        out_specs=pl.BlockSpec((tm, d), lambda i: (i, 0)),
        out_shape=jax.ShapeDtypeStruct((n, d), out_dtype),
        compiler_params=_params("arbitrary"),
        name="rmsnorm",
    )(xa, xb, g.reshape(1, d))


def _glu_kernel(h_ref, wa_ref, wg_ref, ba_ref, bg_ref, o_ref):
    h = h_ref[...]
    a = jnp.dot(h, wa_ref[...], preferred_element_type=F32) + ba_ref[...]
    g = jnp.dot(h, wg_ref[...], preferred_element_type=F32) + bg_ref[...]
    o_ref[...] = (a * jax.nn.sigmoid(g)).astype(o_ref.dtype)


def _glu_matmul(h, w, b):
    n, k = h.shape
    d = w.shape[1] // 2
    tm, tn = _tile(n, MM_TM), _tile(d, MM_TN)
    nj = d // tn
    b2 = b.reshape(1, 2 * d)
    return pl.pallas_call(
        _glu_kernel,
        grid=(n // tm, nj),
        in_specs=[pl.BlockSpec((tm, k), lambda i, j: (i, 0)),
                  pl.BlockSpec((k, tn), lambda i, j: (0, j)),
                  pl.BlockSpec((k, tn), lambda i, j: (0, j + nj)),
                  pl.BlockSpec((1, tn), lambda i, j: (0, j)),
                  pl.BlockSpec((1, tn), lambda i, j: (0, j + nj))],
        out_specs=pl.BlockSpec((tm, tn), lambda i, j: (i, j)),
        out_shape=jax.ShapeDtypeStruct((n, d), BF16),
        compiler_params=_params("parallel", "arbitrary"),
        name="pw1_glu",
    )(h, w, w, b2, b2)


def _mm_kernel(a_ref, w_ref, o_ref):
    o_ref[...] = jnp.dot(a_ref[...], w_ref[...], preferred_element_type=F32).astype(o_ref.dtype)


def _mm_bias_res2_kernel(a_ref, w_ref, b_ref, ra_ref, rb_ref, o_ref, *, nb_first):
    y = jnp.dot(a_ref[...], w_ref[...], preferred_element_type=F32) + b_ref[...]
    i = pl.program_id(0)

    @pl.when(i < nb_first)
    def _():
        o_ref[...] = ra_ref[...] + y

    @pl.when(i >= nb_first)
    def _():
        o_ref[...] = rb_ref[...] + y


def _matmul_bias_res_stacked(a, w, bias, res_a, res_b, *, name):
    n, k = a.shape
    m = w.shape[1]
    tm, tn = _tile(res_a.shape[0], MM_TM), _tile(m, MM_TN)
    assert res_a.shape[0] + res_b.shape[0] == n and res_b.shape[0] % tm == 0
    nbf, nj = res_a.shape[0] // tm, m // tn
    o_spec = pl.BlockSpec((tm, tn), lambda i, j: (i, j))
    ra_spec = pl.BlockSpec((tm, tn), lambda i, j: (jnp.minimum(i, nbf - 1), jnp.where(i < nbf, j, nj - 1)))
    rb_spec = pl.BlockSpec((tm, tn), lambda i, j: (jnp.maximum(i - nbf, 0), jnp.where(i >= nbf, j, 0)))
    return pl.pallas_call(
        functools.partial(_mm_bias_res2_kernel, nb_first=nbf),
        grid=(n // tm, nj),
        in_specs=[pl.BlockSpec((tm, k), lambda i, j: (i, 0)),
                  pl.BlockSpec((k, tn), lambda i, j: (0, j)),
                  pl.BlockSpec((1, tn), lambda i, j: (0, j)),
                  ra_spec, rb_spec],
        out_specs=o_spec,
        out_shape=jax.ShapeDtypeStruct((n, m), F32),
        compiler_params=_params("arbitrary", "arbitrary"),
        name=name,
    )(a, w, bias.reshape(1, m), res_a, res_b)


def _mm_res_kernel(a_ref, w_ref, r_ref, o_ref):
    o_ref[...] = r_ref[...] + jnp.dot(a_ref[...], w_ref[...], preferred_element_type=F32)


def _matmul(a, w, *, res=None, out_dtype=F32, name="matmul"):
    n, k = a.shape
    m = w.shape[1]
    tm, tn = _tile(n, MM_TM), _tile(m, MM_TN)
    a_spec = pl.BlockSpec((tm, k), lambda i, j: (i, 0))
    w_spec = pl.BlockSpec((k, tn), lambda i, j: (0, j))
    o_spec = pl.BlockSpec((tm, tn), lambda i, j: (i, j))
    if res is None:
        kern, specs, args = _mm_kernel, [a_spec, w_spec], (a, w)
    else:
        kern, specs, args = _mm_res_kernel, [a_spec, w_spec, o_spec], (a, w, res)
    return pl.pallas_call(
        kern,
        grid=(n // tm, m // tn),
        in_specs=specs,
        out_specs=o_spec,
        out_shape=jax.ShapeDtypeStruct((n, m), out_dtype),
        compiler_params=_params("parallel", "arbitrary"),
        name=name,
    )(*args)


def _conv_kernel(prev_ref, cur_ref, next_ref, w_ref, bdw_ref, g_ref, b_ref, o_ref,
                 win_ref, acc_ref, *, ts, n_tblk, d):
    i = pl.program_id(1)
    halo = CONV_HALO
    n_cb = d // LANES
    has_prev = i > 0
    has_next = i < n_tblk - 1
    for cb in range(n_cb):
        sl = slice(cb * LANES, (cb + 1) * LANES)
        win_ref[cb, 0:halo, :] = jnp.where(has_prev, prev_ref[:, sl].astype(F32), 0.0)
        win_ref[cb, halo:halo + ts, :] = cur_ref[:, sl].astype(F32)
        win_ref[cb, halo + ts:halo + ts + halo, :] = jnp.where(has_next, next_ref[:, sl].astype(F32), 0.0)

    base = halo - CONV_WIDTH // 2
    span = 2 * CONV_RB
    starts = [sp * span + par for sp in range(ts // span) for par in (0, 1)]

    def col_body(cb, carry):
        accs = [jnp.zeros((CONV_RB, LANES), F32) for _ in starts]
        for k in range(CONV_WIDTH):
            wk = w_ref[cb, k:k + 1, :]
            for j, st in enumerate(starts):
                accs[j] = accs[j] + win_ref[cb, pl.ds(st + base + k, CONV_RB, stride=2), :] * wk
        bias = bdw_ref[cb]
        for j, st in enumerate(starts):
            acc_ref[cb, pl.ds(st, CONV_RB, stride=2), :] = accs[j] + bias
        return carry

    lax.fori_loop(0, n_cb, col_body, 0)

    inv_d = 1.0 / d

    def ln_body(rb, carry):
        r0 = pl.multiple_of(rb * LN_RB, LN_RB)
        y = acc_ref[:, pl.ds(r0, LN_RB), :]
        mu = jnp.sum(jnp.sum(y, axis=0), axis=-1, keepdims=True) * inv_d
        yc = y - mu[None]
        var = jnp.sum(jnp.sum(yc * yc, axis=0), axis=-1, keepdims=True) * inv_d
        z = yc * lax.rsqrt(var + LN_EPS)[None] * g_ref[...] + b_ref[...]
        z = (z * jax.nn.sigmoid(z)).astype(o_ref.dtype)
        for cb in range(n_cb):
            o_ref[pl.ds(r0, LN_RB), cb * LANES:(cb + 1) * LANES] = z[cb]
        return carry

    lax.fori_loop(0, ts // LN_RB, ln_body, 0)


def _conv_ln_swish(u, w_dw, b_dw, ln_g, ln_b, seq):
    n, d = u.shape
    n_seq = n // seq
    ts = _tile(seq, CONV_TS)
    n_tblk = seq // ts
    halo = CONV_HALO
    n_cb = d // LANES
    assert halo >= CONV_WIDTH // 2 and ts % halo == 0 and ts % (2 * CONV_RB) == 0 and d % LANES == 0
    hb = ts // halo
    last_hblk = n // halo - 1

    def prev_map(b, i):
        return (jnp.maximum((b * n_tblk + i) * hb - 1, 0), 0)

    def next_map(b, i):
        return (jnp.minimum((b * n_tblk + i + 1) * hb, last_hblk), 0)

    slab = lambda v: v.reshape(-1, n_cb, LANES).transpose(1, 0, 2)
    vec = lambda b, i: (0, 0, 0)
    kern = functools.partial(_conv_kernel, ts=ts, n_tblk=n_tblk, d=d)
    return pl.pallas_call(
        kern,
        grid=(n_seq, n_tblk),
        in_specs=[pl.BlockSpec((halo, d), prev_map),
                  pl.BlockSpec((ts, d), lambda b, i: (b * n_tblk + i, 0)),
                  pl.BlockSpec((halo, d), next_map),
                  pl.BlockSpec((n_cb, CONV_WIDTH, LANES), vec),
                  pl.BlockSpec((n_cb, 1, LANES), vec),
                  pl.BlockSpec((n_cb, 1, LANES), vec),
                  pl.BlockSpec((n_cb, 1, LANES), vec)],
        out_specs=pl.BlockSpec((ts, d), lambda b, i: (b * n_tblk + i, 0)),
        out_shape=jax.ShapeDtypeStruct((n, d), BF16),
        scratch_shapes=[pltpu.VMEM((n_cb, ts + 2 * halo, LANES), F32),
                        pltpu.VMEM((n_cb, ts, LANES), F32)],
        compiler_params=_params("parallel", "arbitrary"),
        name="dwconv_ln_swish",
    )(u, u, u, slab(w_dw), slab(b_dw), slab(ln_g), slab(ln_b))


def _natten_plan(rows):
    kh = min(WIN_H, rows)
    assert rows % NA_QROWS == 0 and rows >= NA_KROWS
    n_groups = rows // NA_QROWS
    k_start = np.clip(np.arange(n_groups) * NA_QROWS - kh // 2, 0, rows - NA_KROWS)
    patterns, slab_of = [], []
    for g in range(n_groups):
        pat = np.full((NA_QROWS, NA_KROWS), -1, np.int64)
        for qi in range(NA_QROWS):
            r = g * NA_QROWS + qi
            rs = int(np.clip(r - kh // 2, 0, rows - kh))
            for j in range(NA_KROWS):
                kr = int(k_start[g]) + j
                if rs <= kr < rs + kh:
                    pat[qi, j] = kr - r + WIN_H - 1
            assert (pat[qi] >= 0).sum() == kh
        key = pat.tobytes()
        if key not in [p.tobytes() for p in patterns]:
            patterns.append(pat)
        slab_of.append([p.tobytes() for p in patterns].index(key))
    return k_start.astype(np.int32), np.asarray(slab_of, np.int32), np.stack(patterns)


def _natten_bias(rpb, patterns):
    w = GRID_W
    qc = np.arange(w)
    cs = np.clip(qc - WIN_W // 2, 0, w - WIN_W)
    kc = np.arange(w)
    col_mask = (kc[None, :] >= cs[:, None]) & (kc[None, :] < cs[:, None] + WIN_W)
    col_idx = np.clip(kc[None, :] - qc[:, None] + WIN_W - 1, 0, 2 * WIN_W - 2)
    heads = rpb.shape[0]
    blocks = jnp.where(jnp.asarray(col_mask)[None, None], rpb.astype(F32)[:, :, col_idx] * LOG2_E, NEG_INF)
    masked = jnp.full((heads, w, w), NEG_INF, F32)
    slabs = []
    for pat in patterns:
        rows_ = [jnp.concatenate([blocks[:, int(ri)] if ri >= 0 else masked for ri in pat_q], axis=-1)
                 for pat_q in pat]
        slabs.append(jnp.concatenate(rows_, axis=-2))
    return jnp.stack(slabs, axis=1)


def _natten_kernel(kstart_ref, slab_ref, q_ref, k_ref, v_ref, bias_ref, o_ref, *, n_groups, scale):
    gq = NA_QROWS * GRID_W
    gk = NA_KROWS * GRID_W

    def body(g, carry):
        q0 = pl.multiple_of(g * gq, gq)
        k0 = pl.multiple_of(kstart_ref[g] * GRID_W, GRID_W)
        q = q_ref[pl.ds(q0, gq), :]
        kb = k_ref[pl.ds(k0, gk), :]
        vb = v_ref[pl.ds(k0, gk), :]
        s = lax.dot_general(q, kb, (((1,), (1,)), ((), ())), preferred_element_type=F32) * (scale * LOG2_E)
        s = s + bias_ref[0, slab_ref[g]]
        m = jnp.max(s, axis=-1, keepdims=True)
        e = jnp.exp2(s - m)
        inv = 1.0 / jnp.sum(e, axis=-1, keepdims=True)
        o = jnp.dot(e.astype(BF16), vb, preferred_element_type=F32) * inv
        o_ref[pl.ds(q0, gq), :] = o.astype(o_ref.dtype)
        return carry

    lax.fori_loop(0, n_groups, body, 0, unroll=NA_UNROLL)


def _natten(qkv, rpb, seq):
    n, d3 = qkv.shape
    d = d3 // 3
    heads = d // NA_HEAD_DIM
    n_seq = n // seq
    rows = seq // GRID_W
    k_start, slab_of, patterns = _natten_plan(rows)
    bias = _natten_bias(rpb, patterns)
    n_slabs = patterns.shape[0]
    gq, gk = NA_QROWS * GRID_W, NA_KROWS * GRID_W
    kern = functools.partial(_natten_kernel, n_groups=rows // NA_QROWS, scale=NA_HEAD_DIM ** -0.5)
    grid_spec = pltpu.PrefetchScalarGridSpec(
        num_scalar_prefetch=2,
        grid=(heads, n_seq),
        in_specs=[pl.BlockSpec((seq, NA_HEAD_DIM), lambda h, b, ks, sl: (b, h)),
                  pl.BlockSpec((seq, NA_HEAD_DIM), lambda h, b, ks, sl: (b, heads + h)),
                  pl.BlockSpec((seq, NA_HEAD_DIM), lambda h, b, ks, sl: (b, 2 * heads + h)),
                  pl.BlockSpec((1, n_slabs, gq, gk), lambda h, b, ks, sl: (h, 0, 0, 0))],
        out_specs=pl.BlockSpec((seq, NA_HEAD_DIM), lambda h, b, ks, sl: (b, h)),
    )
    return pl.pallas_call(
        kern,
        grid_spec=grid_spec,
        out_shape=jax.ShapeDtypeStruct((n, d), BF16),
        compiler_params=_params("parallel", "arbitrary"),
        name="natten",
    )(jnp.asarray(k_start), jnp.asarray(slab_of), qkv, qkv, qkv, bias)


def _router_kernel(x_ref, g_ref, whi_ref, wlo_ref, b_ref, hp_ref, route_ref, *, d):
    x = x_ref[...]
    ms = jnp.mean(x * x, axis=-1, keepdims=True)
    hn = x * lax.rsqrt(ms + RMS_EPS) * g_ref[...]
    hb = hn.astype(BF16)
    hbf = hb.astype(F32)
    hlo = (hn - hbf).astype(BF16)
    logits = (jnp.dot(hb, whi_ref[...], preferred_element_type=F32)
              + (jnp.dot(hb, wlo_ref[...], preferred_element_type=F32)
                 + jnp.dot(hlo, whi_ref[...], preferred_element_type=F32))) + b_ref[...]

    lane = lax.broadcasted_iota(jnp.int32, logits.shape, 1)
    big = jnp.int32(ROUTE_LANES)
    is_grp = lane < N_GROUPS
    gl = jnp.where(is_grp, logits, NEG_INF)
    gmax = jnp.max(gl, axis=-1, keepdims=True)
    gsum = jnp.sum(jnp.where(is_grp, jnp.exp(gl - gmax), 0.0), axis=-1, keepdims=True)
    p_top = 1.0 / gsum
    g_top = jnp.min(jnp.where(is_grp & (gl == gmax), lane, big), axis=-1, keepdims=True)
    lo = N_GROUPS + g_top * EXPERTS_PER_GROUP
    in_grp = (lane >= lo) & (lane < lo + EXPERTS_PER_GROUP)
    sl = jnp.where(in_grp, logits, NEG_INF)
    v1 = jnp.max(sl, axis=-1, keepdims=True)
    i1 = jnp.min(jnp.where(in_grp & (sl == v1), lane, big), axis=-1, keepdims=True)
    rest = in_grp & (lane != i1)
    sl2 = jnp.where(rest, logits, NEG_INF)
    v2 = jnp.max(sl2, axis=-1, keepdims=True)
    i2 = jnp.min(jnp.where(rest & (sl2 == v2), lane, big), axis=-1, keepdims=True)
    ex = jnp.exp(v2 - v1)
    den = 1.0 + ex
    gate1 = p_top * (1.0 / den)
    gate2 = p_top * (ex / den)
    e1 = (i1 - N_GROUPS).astype(F32)
    e2 = (i2 - N_GROUPS).astype(F32)
    route_ref[...] = jnp.where(lane == 0, e1,
                               jnp.where(lane == 1, e2,
                                         jnp.where(lane == 2, gate1,
                                                   jnp.where(lane == 3, gate2, 0.0))))

    bits = lax.bitcast_convert_type(hbf, jnp.uint32)
    half = d // 2
    hp_ref[...] = bits[:, :half] | (bits[:, half:] >> 16)


def _router(x, g, w_hi, w_lo, b):
    n, d = x.shape
    tm = _tile(n, ROUTER_TM)
    kern = functools.partial(_router_kernel, d=d)
    vec = lambda i: (0, 0)
    return pl.pallas_call(
        kern,
        grid=(n // tm,),
        in_specs=[pl.BlockSpec((tm, d), lambda i: (i, 0)),
                  pl.BlockSpec((1, d), vec),
                  pl.BlockSpec((d, ROUTE_LANES), vec),
                  pl.BlockSpec((d, ROUTE_LANES), vec),
                  pl.BlockSpec((1, ROUTE_LANES), vec)],
        out_specs=[pl.BlockSpec((tm, d // 2), lambda i: (i, 0)),
                   pl.BlockSpec((tm, ROUTE_LANES), lambda i: (i, 0))],
        out_shape=[jax.ShapeDtypeStruct((n, d // 2), jnp.uint32),
                   jax.ShapeDtypeStruct((n, ROUTE_LANES), F32)],
        compiler_params=_params("parallel"),
        name="moe_router",
    )(x, g.reshape(1, d), w_hi, w_lo, b)


def _row_copy(src_ref, src_row, dst_ref, dst_row, sem):
    return pltpu.make_async_copy(src_ref.at[pl.ds(src_row, 1)], dst_ref.at[pl.ds(dst_row, 1)], sem)


def _expert_kernel(ce_ref, nv_ref, nu_ref, tok_ref, hp_ref, wg_ref, wu_ref, wd_ref, y_ref,
                   xraw_ref, xb_ref, h_ref, wgb_ref, wub_ref, wdb_ref, sem, *, rows, n_f, tf):
    c = pl.program_id(0)
    s = pl.program_id(1)
    sub, sub2 = EXPERT_SUB, 2 * EXPERT_SUB
    half = xraw_ref.shape[1]
    n_blk = jnp.where(c >= 1, (nv_ref[jnp.maximum(c - 1, 0)] + sub - 1) // sub, 0)
    n_blk_next = (nv_ref[c] + sub - 1) // sub

    @pl.when(s == 0)
    def _():
        def wait_block(i, carry):
            r0 = pl.multiple_of(i * sub, sub)
            pltpu.make_async_copy(hp_ref.at[pl.ds(0, sub)], xraw_ref.at[pl.ds(r0, sub)], sem).wait()
            return carry

        def unpack_block(i, carry):
            r0 = pl.multiple_of(i * sub, sub)
            w = xraw_ref[pl.ds(r0, sub), :]
            xb_ref[pl.ds(r0, sub), :half] = lax.bitcast_convert_type(w & jnp.uint32(0xFFFF0000), F32).astype(BF16)
            xb_ref[pl.ds(r0, sub), half:] = lax.bitcast_convert_type(w << 16, F32).astype(BF16)
            return carry

        def gather_block(i, carry):
            r0 = i * sub
            for j in range(sub):
                _row_copy(hp_ref, tok_ref[0, 0, r0 + j], xraw_ref, r0 + j, sem).start()
            return carry

        lax.fori_loop(0, n_blk, wait_block, 0)
        lax.fori_loop(0, n_blk, unpack_block, 0)
        lax.fori_loop(0, n_blk_next, gather_block, 0)

    @pl.when((s < n_f) & (n_blk > 0))
    def _():
        wgb_ref[...] = wg_ref[0, 0].astype(BF16)
        wub_ref[...] = wu_ref[0, 0].astype(BF16)
        col = pl.multiple_of(s * tf, tf)

        def block(i, carry):
            r0 = pl.multiple_of(i * sub, sub)
            xb = xb_ref[pl.ds(r0, sub), :]
            g = jnp.dot(xb, wgb_ref[...], preferred_element_type=F32)
            u = jnp.dot(xb, wub_ref[...], preferred_element_type=F32)
            h_ref[pl.ds(r0, sub), pl.ds(col, tf)] = (g * jax.nn.sigmoid(g) * u).astype(BF16)
            return carry

        lax.fori_loop(0, n_blk, block, 0)

        @pl.when(n_blk % 2 == 1)
        def _():
            r0 = pl.multiple_of(n_blk * sub, sub)
            h_ref[pl.ds(r0, sub), pl.ds(col, tf)] = jnp.zeros((sub, tf), BF16)

    @pl.when((c >= 1) & (s >= n_f))
    def _():
        n_blk2 = (n_blk + 1) // 2

        @pl.when(n_blk2 > 0)
        def _():
            wdb_ref[...] = wd_ref[0, 0].astype(BF16)

        def block(i, carry):
            r0 = pl.multiple_of(i * sub2, sub2)
            y_ref[pl.ds(r0, sub2), :] = jnp.dot(h_ref[pl.ds(r0, sub2), :], wdb_ref[...],
                                               preferred_element_type=F32)
            return carry

        def zero_block(i, carry):
            r0 = pl.multiple_of(i * sub2, sub2)
            y_ref[pl.ds(r0, sub2), :] = jnp.zeros((sub2, y_ref.shape[1]), F32)
            return carry

        lax.fori_loop(0, n_blk2, block, 0)
        lax.fori_loop(n_blk2, rows // sub2, zero_block, 0)


def _experts(hp, tok_chunks, chunk_expert, chunk_valid, n_used, w_gate, w_up, w_down, layer):
    n, half = hp.shape
    d = 2 * half
    ff = w_gate.shape[-1]
    n_max, rows = tok_chunks.shape
    tf, tn = _tile(ff, EXPERT_TF), _tile(d, EXPERT_TN)
    n_f, n_n = ff // tf, d // tn
    assert rows % (2 * EXPERT_SUB) == 0 and chunk_valid.shape == (n_max + 1,)

    def chunk(c, nu):
        return jnp.clip(c - 1, 0, nu[0] - 1)

    def active(c, nu):
        return (c >= 1) & (c - 1 < nu[0])

    def gate_map(c, s, ce, nv, nu):
        nxt = jnp.clip(c, 0, nu[0] - 1)
        in_p1 = active(c, nu) & (s < n_f)
        has_next = c < nu[0]
        e = jnp.where(in_p1, ce[chunk(c, nu)], jnp.where(has_next, ce[nxt], ce[chunk(c, nu)]))
        t = jnp.where(in_p1, s, jnp.where(has_next, 0, n_f - 1))
        return (layer, e, 0, t)

    def down_map(c, s, ce, nv, nu):
        t = jnp.where(active(c, nu), jnp.maximum(s - n_f, 0), jnp.where(c == 0, 0, n_n - 1))
        return (layer, ce[chunk(c, nu)], 0, t)

    def out_map(c, s, ce, nv, nu):
        return (jnp.maximum(c - 1, 0), jnp.where(c == 0, 0, jnp.maximum(s - n_f, 0)))

    kern = functools.partial(_expert_kernel, rows=rows, n_f=n_f, tf=tf)
    grid_spec = pltpu.PrefetchScalarGridSpec(
        num_scalar_prefetch=3,
        grid=(n_max + 1, n_f + n_n),
        in_specs=[pl.BlockSpec((1, 1, rows), lambda c, s, ce, nv, nu: (jnp.minimum(c, n_max - 1), 0, 0),
                               memory_space=pltpu.SMEM),
                  pl.BlockSpec(memory_space=pl.ANY),
                  pl.BlockSpec((1, 1, d, tf), gate_map),
                  pl.BlockSpec((1, 1, d, tf), gate_map),
                  pl.BlockSpec((1, 1, ff, tn), down_map)],
        out_specs=pl.BlockSpec((rows, tn), out_map),
        scratch_shapes=[pltpu.VMEM((rows, half), jnp.uint32),
                        pltpu.VMEM((rows, d), BF16),
                        pltpu.VMEM((rows, ff), BF16),
                        pltpu.VMEM((d, tf), BF16),
                        pltpu.VMEM((d, tf), BF16),
                        pltpu.VMEM((ff, tn), BF16),
                        pltpu.SemaphoreType.DMA(())],
    )
    return pl.pallas_call(
        kern,
        grid_spec=grid_spec,
        out_shape=jax.ShapeDtypeStruct((n_max * rows, d), F32),
        compiler_params=_params("arbitrary", "arbitrary"),
        name="moe_experts",
    )(chunk_expert, chunk_valid, n_used.reshape(1), tok_chunks.reshape(n_max, 1, rows), hp, w_gate, w_up, w_down)


def _combine_kernel(pos_ref, posn_ref, x_ref, route_ref, gn_ref, y_ref, o0_ref, o1_ref, ybuf_ref, sems,
                    *, tc, nb, nb_first, last):
    i = pl.program_id(0)
    slot = i & 1

    def issue(p_ref, to_slot):
        def body(a, carry):
            dst = (a & 1) * tc + (a >> 1)
            pltpu.make_async_copy(y_ref.at[pl.ds(p_ref[0, 0, a], 1)],
                                  ybuf_ref.at[to_slot, pl.ds(dst, 1)], sems.at[to_slot]).start()
            return carry

        lax.fori_loop(0, 2 * tc, body, 0, unroll=8)

    @pl.when(i == 0)
    def _():
        issue(pos_ref, 0)

    @pl.when(i + 1 < nb)
    def _():
        issue(posn_ref, 1 - slot)

    pltpu.make_async_copy(y_ref.at[pl.ds(0, 2 * tc)], ybuf_ref.at[slot], sems.at[slot]).wait()

    route = route_ref[...]
    yb = ybuf_ref.at[slot]
    x = x_ref[...] + (route[:, 2:3] * yb[0:tc, :] + route[:, 3:4] * yb[tc:2 * tc, :])
    ms = jnp.mean(x * x, axis=-1, keepdims=True)
    hn = x * lax.rsqrt(ms + RMS_EPS) * gn_ref[...]
    if last:
        @pl.when(i < nb_first)
        def _():
            o0_ref[...] = hn

        @pl.when(i >= nb_first)
        def _():
            o1_ref[...] = hn
    else:
        o0_ref[...] = x
        o1_ref[...] = hn.astype(o1_ref.dtype)


def _combine(x, y, pos, route, g_next, *, n_first, last):
    n, d = x.shape
    tc = _tile(n, COMBINE_TC)
    nb = n // tc
    assert n_first % tc == 0
    nb_first = n_first // tc
    kern = functools.partial(_combine_kernel, tc=tc, nb=nb, nb_first=nb_first, last=last)
    row = pl.BlockSpec((tc, d), lambda i: (i, 0))
    if last:
        out_specs = [pl.BlockSpec((tc, d), lambda i: (jnp.minimum(i, nb_first - 1), 0)),
                     pl.BlockSpec((tc, d), lambda i: (jnp.maximum(i - nb_first, 0), 0))]
        out_shape = [jax.ShapeDtypeStruct((n_first, d), F32), jax.ShapeDtypeStruct((n - n_first, d), F32)]
    else:
        out_specs = [row, row]
        out_shape = [jax.ShapeDtypeStruct((n, d), F32), jax.ShapeDtypeStruct((n, d), BF16)]
    pos3 = pos.reshape(nb, 1, 2 * tc)
    return pl.pallas_call(
        kern,
        grid=(nb,),
        in_specs=[pl.BlockSpec((1, 1, 2 * tc), lambda i: (i, 0, 0), memory_space=pltpu.SMEM),
                  pl.BlockSpec((1, 1, 2 * tc), lambda i: (jnp.minimum(i + 1, nb - 1), 0, 0),
                               memory_space=pltpu.SMEM),
                  row,
                  pl.BlockSpec((tc, ROUTE_LANES), lambda i: (i, 0)),
                  pl.BlockSpec((1, d), lambda i: (0, 0)),
                  pl.BlockSpec(memory_space=pl.ANY)],
        out_specs=out_specs,
        out_shape=out_shape,
        scratch_shapes=[pltpu.VMEM((2, 2 * tc, d), F32), pltpu.SemaphoreType.DMA((2,))],
        compiler_params=_params("arbitrary"),
        name="moe_combine",
    )(pos3, pos3, x, route, g_next.reshape(1, d), y)


def _dispatch_tables(route, rows):
    n = route.shape[0]
    a = n * TOP_K_INNER
    n_max = a // rows + N_EXPERTS
    e_flat = route[:, :TOP_K_INNER].astype(jnp.int32).reshape(a)
    order = jnp.argsort(e_flat, stable=True).astype(jnp.int32)
    sorted_pos = jnp.argsort(order).astype(jnp.int32)
    counts = jnp.sum((e_flat[:, None] == jnp.arange(N_EXPERTS, dtype=jnp.int32)[None, :]).astype(jnp.int32), axis=0)
    seg_start = jnp.cumsum(counts) - counts
    n_chunks = (counts + rows - 1) // rows
    chunk_end = jnp.cumsum(n_chunks)
    chunk_base = chunk_end - n_chunks
    n_used = chunk_end[-1].astype(jnp.int32)
    cidx = jnp.arange(n_max, dtype=jnp.int32)
    ce = jnp.minimum(jnp.searchsorted(chunk_end, cidx, side='right'), N_EXPERTS - 1).astype(jnp.int32)
    ce = jnp.where(cidx < n_used, ce, ce[n_used - 1])
    ustart = seg_start[ce] + (cidx - chunk_base[ce]) * rows
    tok_sorted = jnp.concatenate([order // TOP_K_INNER, jnp.zeros((rows,), jnp.int32)])
    tok_chunks = jax.vmap(lambda s0: lax.dynamic_slice(tok_sorted, (s0,), (rows,)))(jnp.minimum(ustart, a))
    pos = chunk_base[e_flat] * rows + (sorted_pos - seg_start[e_flat])
    valid = jnp.clip(counts[ce] - (cidx - chunk_base[ce]) * rows, 0, rows)
    valid = jnp.concatenate([jnp.where(cidx < n_used, valid, 0), jnp.zeros((1,), jnp.int32)]).astype(jnp.int32)
    return tok_chunks.astype(jnp.int32), ce, valid, n_used, pos.astype(jnp.int32)


def _moe(x, layer, ffn_norm, moe_w_grp, moe_b_grp, moe_w_sub, moe_b_sub, moe_w_gate, moe_w_up, moe_w_down,
         g_next, *, n_first, last):
    n, d = x.shape
    w_r = jnp.concatenate([moe_w_grp[layer],
                           jnp.transpose(moe_w_sub[layer], (1, 0, 2)).reshape(d, N_EXPERTS)], axis=1)
    w_r = jnp.pad(w_r, ((0, 0), (0, ROUTE_LANES - w_r.shape[1])))
    w_hi = w_r.astype(BF16)
    w_lo = (w_r - w_hi.astype(F32)).astype(BF16)
    b_r = jnp.concatenate([moe_b_grp[layer], moe_b_sub[layer].reshape(N_EXPERTS)])
    b_r = jnp.pad(b_r, (0, ROUTE_LANES - b_r.shape[0])).reshape(1, ROUTE_LANES).astype(F32)

    hp, route = _router(x, ffn_norm[layer], w_hi, w_lo, b_r)
    rows = _tile(n * TOP_K_INNER, EXPERT_ROWS)
    tok_chunks, ce, valid, n_used, pos = _dispatch_tables(route, rows)
    y = _experts(hp, tok_chunks, ce, valid, n_used, moe_w_gate, moe_w_up, moe_w_down, layer)
    return _combine(x, y, pos, route, g_next, n_first=n_first, last=last)


def kernel(x_prompt, x_sample, mix_norm, ffn_norm, final_norm, cv_w_pw1, cv_b_pw1, cv_w_dw, cv_b_dw, cv_ln_g, cv_ln_b, cv_w_pw2, cv_b_pw2, na_w_qkv, na_rpb, na_w_o, moe_w_grp, moe_b_grp, moe_w_sub, moe_b_sub, moe_w_gate, moe_w_up, moe_w_down):
    bp, seq, d = x_prompt.shape
    bs = x_sample.shape[0]
    assert x_sample.shape[1:] == (seq, d) and seq % GRID_W == 0
    n_p = bp * seq
    xp, xs = x_prompt.reshape(n_p, d), x_sample.reshape(bs * seq, d)
    moe_p = (ffn_norm, moe_w_grp, moe_b_grp, moe_w_sub, moe_b_sub, moe_w_gate, moe_w_up, moe_w_down)

    h = _rmsnorm_stacked(xp, xs, mix_norm[0], BF16)
    u = _glu_matmul(h, cv_w_pw1[0].astype(BF16), cv_b_pw1[0])
    v = _conv_ln_swish(u, cv_w_dw[0], cv_b_dw[0], cv_ln_g[0], cv_ln_b[0], seq)
    x = _matmul_bias_res_stacked(v, cv_w_pw2[0].astype(BF16), cv_b_pw2[0], xp, xs, name="pw2_res")
    x, h = _moe(x, 0, *moe_p, mix_norm[1], n_first=n_p, last=False)

    qkv = _matmul(h, na_w_qkv[0].astype(BF16), out_dtype=BF16, name="qkv")
    o = _natten(qkv, na_rpb[0], seq)
    x = _matmul(o, na_w_o[0].astype(BF16), res=x, name="wo_res")
    yp, ys = _moe(x, 1, *moe_p, final_norm, n_first=n_p, last=True)
    return (yp.reshape(bp, seq, d), ys.reshape(bs, seq, d))
```

```python
import functools

import numpy as np
import jax
import jax.numpy as jnp
from jax import lax
from jax.experimental import pallas as pl
from jax.experimental.pallas import tpu as pltpu

GRID_W = 64
CONV_WIDTH = 31
NA_HEAD_DIM = 128
WIN_H = 8
WIN_W = 16
N_GROUPS = 8
EXPERTS_PER_GROUP = 8
N_EXPERTS = N_GROUPS * EXPERTS_PER_GROUP
TOP_K_INNER = 2
RMS_EPS = 1e-6
LN_EPS = 1e-5
NEG_INF = -1e30
LOG2_E = 1.4426950408889634

LANES = 128
BF16_SUBLANES = 16
V7X_VMEM_BUDGET = 58 * 1024 * 1024

MM_TM = 1024
MM_TN = 512
MM_TN_WIDE = 1024
NORM_TM = 256
CONV_TS = 256
CONV_HALO = BF16_SUBLANES
CONV_RB = 64
LN_RB = 32
NA_QROWS = 4
NA_KROWS = NA_QROWS + WIN_H
NA_UNROLL = 4
ROUTER_TM = 256
ROUTE_LANES = LANES
EXPERT_ROWS = 1024
EXPERT_SUB = 256
EXPERT_TF = 256
EXPERT_TN = 512
COMBINE_TC = 256

F32 = jnp.float32
BF16 = jnp.bfloat16


def _tile(n, pref):
    t = min(n, pref)
    assert n % t == 0, (n, pref)
    return t


def _params(*sem):
    return pltpu.CompilerParams(dimension_semantics=sem, vmem_limit_bytes=V7X_VMEM_BUDGET)


def _first_blocks(nb_first):
    return (lambda i: (jnp.minimum(i, nb_first - 1), 0)), (lambda i: (jnp.maximum(i - nb_first, 0), 0))


def _rmsnorm_kernel(xa_ref, xb_ref, g_ref, o_ref, *, nb_first):
    def norm(x_ref):
        x = x_ref[...]
        ms = jnp.mean(x * x, axis=-1, keepdims=True)
        o_ref[...] = (x * lax.rsqrt(ms + RMS_EPS) * g_ref[...]).astype(o_ref.dtype)

    i = pl.program_id(0)
    pl.when(i < nb_first)(lambda: norm(xa_ref))
    pl.when(i >= nb_first)(lambda: norm(xb_ref))


def _rmsnorm_stacked(xa, xb, g, out_dtype):
    na, d = xa.shape
    n = na + xb.shape[0]
    tm = _tile(na, NORM_TM)
    assert xb.shape[0] % tm == 0
    a_map, b_map = _first_blocks(na // tm)
    return pl.pallas_call(
        functools.partial(_rmsnorm_kernel, nb_first=na // tm),
        grid=(n // tm,),
        in_specs=[pl.BlockSpec((tm, d), a_map),
                  pl.BlockSpec((tm, d), b_map),
                  pl.BlockSpec((1, d), lambda i: (0, 0))],
        out_specs=pl.BlockSpec((tm, d), lambda i: (i, 0)),
        out_shape=jax.ShapeDtypeStruct((n, d), out_dtype),
        compiler_params=_params("arbitrary"),
        name="rmsnorm",
    )(xa, xb, g.reshape(1, d))


def _glu_kernel(h_ref, wa_ref, wg_ref, ba_ref, bg_ref, o_ref):
    h = h_ref[...]
    a = jnp.dot(h, wa_ref[...], preferred_element_type=F32) + ba_ref[...]
    g = jnp.dot(h, wg_ref[...], preferred_element_type=F32) + bg_ref[...]
    o_ref[...] = (a * jax.nn.sigmoid(g)).astype(o_ref.dtype)


def _glu_matmul(h, w, b):
    n, k = h.shape
    d = w.shape[1] // 2
    tm, tn = _tile(n, MM_TM), _tile(d, MM_TN)
    nj = d // tn
    b2 = b.reshape(1, 2 * d)
    return pl.pallas_call(
        _glu_kernel,
        grid=(n // tm, nj),
        in_specs=[pl.BlockSpec((tm, k), lambda i, j: (i, 0)),
                  pl.BlockSpec((k, tn), lambda i, j: (0, j)),
                  pl.BlockSpec((k, tn), lambda i, j: (0, j + nj)),
                  pl.BlockSpec((1, tn), lambda i, j: (0, j)),
                  pl.BlockSpec((1, tn), lambda i, j: (0, j + nj))],
        out_specs=pl.BlockSpec((tm, tn), lambda i, j: (i, j)),
        out_shape=jax.ShapeDtypeStruct((n, d), BF16),
        compiler_params=_params("parallel", "arbitrary"),
        name="pw1_glu",
    )(h, w, w, b2, b2)


def _mm_kernel(a_ref, w_ref, o_ref):
    o_ref[...] = jnp.dot(a_ref[...], w_ref[...], preferred_element_type=F32).astype(o_ref.dtype)


def _mm_bias_res2_kernel(a_ref, w_ref, b_ref, ra_ref, rb_ref, o_ref, *, nb_first):
    y = jnp.dot(a_ref[...], w_ref[...], preferred_element_type=F32) + b_ref[...]
    i = pl.program_id(0)

    @pl.when(i < nb_first)
    def _():
        o_ref[...] = ra_ref[...] + y

    @pl.when(i >= nb_first)
    def _():
        o_ref[...] = rb_ref[...] + y


def _matmul_bias_res_stacked(a, w, bias, res_a, res_b, *, name):
    n, k = a.shape
    m = w.shape[1]
    tm, tn = _tile(res_a.shape[0], MM_TM), _tile(m, MM_TN)
    assert res_a.shape[0] + res_b.shape[0] == n and res_b.shape[0] % tm == 0
    nbf, nj = res_a.shape[0] // tm, m // tn
    o_spec = pl.BlockSpec((tm, tn), lambda i, j: (i, j))
    ra_spec = pl.BlockSpec((tm, tn), lambda i, j: (jnp.minimum(i, nbf - 1), jnp.where(i < nbf, j, nj - 1)))
    rb_spec = pl.BlockSpec((tm, tn), lambda i, j: (jnp.maximum(i - nbf, 0), jnp.where(i >= nbf, j, 0)))
    return pl.pallas_call(
        functools.partial(_mm_bias_res2_kernel, nb_first=nbf),
        grid=(n // tm, nj),
        in_specs=[pl.BlockSpec((tm, k), lambda i, j: (i, 0)),
                  pl.BlockSpec((k, tn), lambda i, j: (0, j)),
                  pl.BlockSpec((1, tn), lambda i, j: (0, j)),
                  ra_spec, rb_spec],
        out_specs=o_spec,
        out_shape=jax.ShapeDtypeStruct((n, m), F32),
        compiler_params=_params("arbitrary", "arbitrary"),
        name=name,
    )(a, w, bias.reshape(1, m), res_a, res_b)


def _mm_res_kernel(a_ref, w_ref, r_ref, o_ref):
    o_ref[...] = r_ref[...] + jnp.dot(a_ref[...], w_ref[...], preferred_element_type=F32)


def _matmul(a, w, *, res=None, out_dtype=F32, name="matmul"):
    n, k = a.shape
    m = w.shape[1]
    tm, tn = _tile(n, MM_TM), _tile(m, MM_TN_WIDE)
    a_spec = pl.BlockSpec((tm, k), lambda i, j: (i, 0))
    w_spec = pl.BlockSpec((k, tn), lambda i, j: (0, j))
    o_spec = pl.BlockSpec((tm, tn), lambda i, j: (i, j))
    if res is None:
        kern, specs, args = _mm_kernel, [a_spec, w_spec], (a, w)
    else:
        kern, specs, args = _mm_res_kernel, [a_spec, w_spec, o_spec], (a, w, res)
    return pl.pallas_call(
        kern,
        grid=(n // tm, m // tn),
        in_specs=specs,
        out_specs=o_spec,
        out_shape=jax.ShapeDtypeStruct((n, m), out_dtype),
        compiler_params=_params("parallel", "arbitrary"),
        name=name,
    )(*args)


def _conv_kernel(prev_ref, cur_ref, next_ref, w_ref, bdw_ref, g_ref, b_ref, o_ref,
                 win_ref, acc_ref, *, ts, n_tblk, d):
    i = pl.program_id(1)
    halo = CONV_HALO
    n_cb = d // LANES
    has_prev = i > 0
    has_next = i < n_tblk - 1
    for cb in range(n_cb):
        sl = slice(cb * LANES, (cb + 1) * LANES)
        win_ref[cb, 0:halo, :] = jnp.where(has_prev, prev_ref[:, sl].astype(F32), 0.0)
        win_ref[cb, halo:halo + ts, :] = cur_ref[:, sl].astype(F32)
        win_ref[cb, halo + ts:halo + ts + halo, :] = jnp.where(has_next, next_ref[:, sl].astype(F32), 0.0)

    base = halo - CONV_WIDTH // 2
    span = 2 * CONV_RB
    starts = [sp * span + par for sp in range(ts // span) for par in (0, 1)]

    def col_body(cb, carry):
        accs = [jnp.zeros((CONV_RB, LANES), F32) for _ in starts]
        for k in range(CONV_WIDTH):
            wk = w_ref[cb, k:k + 1, :]
            for j, st in enumerate(starts):
                accs[j] = accs[j] + win_ref[cb, pl.ds(st + base + k, CONV_RB, stride=2), :] * wk
        bias = bdw_ref[cb]
        for j, st in enumerate(starts):
            acc_ref[cb, pl.ds(st, CONV_RB, stride=2), :] = accs[j] + bias
        return carry

    lax.fori_loop(0, n_cb, col_body, 0)

    inv_d = 1.0 / d

    def ln_body(rb, carry):
        r0 = pl.multiple_of(rb * LN_RB, LN_RB)
        y = acc_ref[:, pl.ds(r0, LN_RB), :]
        mu = jnp.sum(jnp.sum(y, axis=0), axis=-1, keepdims=True) * inv_d
        yc = y - mu[None]
        var = jnp.sum(jnp.sum(yc * yc, axis=0), axis=-1, keepdims=True) * inv_d
        z = yc * lax.rsqrt(var + LN_EPS)[None] * g_ref[...] + b_ref[...]
        z = (z * jax.nn.sigmoid(z)).astype(o_ref.dtype)
        for cb in range(n_cb):
            o_ref[pl.ds(r0, LN_RB), cb * LANES:(cb + 1) * LANES] = z[cb]
        return carry

    lax.fori_loop(0, ts // LN_RB, ln_body, 0)


def _conv_ln_swish(u, w_dw, b_dw, ln_g, ln_b, seq):
    n, d = u.shape
    n_seq = n // seq
    ts = _tile(seq, CONV_TS)
    n_tblk = seq // ts
    halo = CONV_HALO
    n_cb = d // LANES
    assert halo >= CONV_WIDTH // 2 and ts % halo == 0 and ts % (2 * CONV_RB) == 0 and d % LANES == 0
    hb = ts // halo
    last_hblk = n // halo - 1

    def prev_map(b, i):
        return (jnp.maximum((b * n_tblk + i) * hb - 1, 0), 0)

    def next_map(b, i):
        return (jnp.minimum((b * n_tblk + i + 1) * hb, last_hblk), 0)

    slab = lambda v: v.reshape(-1, n_cb, LANES).transpose(1, 0, 2)
    vec = lambda b, i: (0, 0, 0)
    kern = functools.partial(_conv_kernel, ts=ts, n_tblk=n_tblk, d=d)
    return pl.pallas_call(
        kern,
        grid=(n_seq, n_tblk),
        in_specs=[pl.BlockSpec((halo, d), prev_map),
                  pl.BlockSpec((ts, d), lambda b, i: (b * n_tblk + i, 0)),
                  pl.BlockSpec((halo, d), next_map),
                  pl.BlockSpec((n_cb, CONV_WIDTH, LANES), vec),
                  pl.BlockSpec((n_cb, 1, LANES), vec),
                  pl.BlockSpec((n_cb, 1, LANES), vec),
                  pl.BlockSpec((n_cb, 1, LANES), vec)],
        out_specs=pl.BlockSpec((ts, d), lambda b, i: (b * n_tblk + i, 0)),
        out_shape=jax.ShapeDtypeStruct((n, d), BF16),
        scratch_shapes=[pltpu.VMEM((n_cb, ts + 2 * halo, LANES), F32),
                        pltpu.VMEM((n_cb, ts, LANES), F32)],
        compiler_params=_params("parallel", "arbitrary"),
        name="dwconv_ln_swish",
    )(u, u, u, slab(w_dw), slab(b_dw), slab(ln_g), slab(ln_b))


def _natten_plan(rows):
    kh = min(WIN_H, rows)
    assert rows % NA_QROWS == 0 and rows >= NA_KROWS
    n_groups = rows // NA_QROWS
    k_start = np.clip(np.arange(n_groups) * NA_QROWS - kh // 2, 0, rows - NA_KROWS)
    patterns, slab_of = [], []
    for g in range(n_groups):
        pat = np.full((NA_QROWS, NA_KROWS), -1, np.int64)
        for qi in range(NA_QROWS):
            r = g * NA_QROWS + qi
            rs = int(np.clip(r - kh // 2, 0, rows - kh))
            for j in range(NA_KROWS):
                kr = int(k_start[g]) + j
                if rs <= kr < rs + kh:
                    pat[qi, j] = kr - r + WIN_H - 1
            assert (pat[qi] >= 0).sum() == kh
        key = pat.tobytes()
        if key not in [p.tobytes() for p in patterns]:
            patterns.append(pat)
        slab_of.append([p.tobytes() for p in patterns].index(key))
    return k_start.astype(np.int32), np.asarray(slab_of, np.int32), np.stack(patterns)


def _natten_bias(rpb, patterns):
    w = GRID_W
    qc = np.arange(w)
    cs = np.clip(qc - WIN_W // 2, 0, w - WIN_W)
    kc = np.arange(w)
    col_mask = (kc[None, :] >= cs[:, None]) & (kc[None, :] < cs[:, None] + WIN_W)
    col_idx = np.clip(kc[None, :] - qc[:, None] + WIN_W - 1, 0, 2 * WIN_W - 2)
    heads = rpb.shape[0]
    blocks = jnp.where(jnp.asarray(col_mask)[None, None], rpb.astype(F32)[:, :, col_idx] * LOG2_E, NEG_INF)
    masked = jnp.full((heads, w, w), NEG_INF, F32)
    slabs = []
    for pat in patterns:
        rows_ = [jnp.concatenate([blocks[:, int(ri)] if ri >= 0 else masked for ri in pat_q], axis=-1)
                 for pat_q in pat]
        slabs.append(jnp.concatenate(rows_, axis=-2))
    return jnp.stack(slabs, axis=1)


def _natten_kernel(kstart_ref, slab_ref, q_ref, k_ref, v_ref, bias_ref, o_ref, *, n_groups, scale):
    gq = NA_QROWS * GRID_W
    gk = NA_KROWS * GRID_W

    def body(g, carry):
        q0 = pl.multiple_of(g * gq, gq)
        k0 = pl.multiple_of(kstart_ref[g] * GRID_W, GRID_W)
        q = q_ref[pl.ds(q0, gq), :]
        kb = k_ref[pl.ds(k0, gk), :]
        vb = v_ref[pl.ds(k0, gk), :]
        s = lax.dot_general(q, kb, (((1,), (1,)), ((), ())), preferred_element_type=F32) * (scale * LOG2_E)
        s = s + bias_ref[0, slab_ref[g]]
        m = jnp.max(s, axis=-1, keepdims=True)
        e = jnp.exp2(s - m)
        inv = 1.0 / jnp.sum(e, axis=-1, keepdims=True)
        o = jnp.dot(e.astype(BF16), vb, preferred_element_type=F32) * inv
        o_ref[pl.ds(q0, gq), :] = o.astype(o_ref.dtype)
        return carry

    lax.fori_loop(0, n_groups, body, 0, unroll=NA_UNROLL)


def _natten(qkv, rpb, seq):
    n, d3 = qkv.shape
    d = d3 // 3
    heads = d // NA_HEAD_DIM
    n_seq = n // seq
    rows = seq // GRID_W
    k_start, slab_of, patterns = _natten_plan(rows)
    bias = _natten_bias(rpb, patterns)
    n_slabs = patterns.shape[0]
    gq, gk = NA_QROWS * GRID_W, NA_KROWS * GRID_W
    kern = functools.partial(_natten_kernel, n_groups=rows // NA_QROWS, scale=NA_HEAD_DIM ** -0.5)
    grid_spec = pltpu.PrefetchScalarGridSpec(
        num_scalar_prefetch=2,
        grid=(heads, n_seq),
        in_specs=[pl.BlockSpec((seq, NA_HEAD_DIM), lambda h, b, ks, sl: (b, h)),
                  pl.BlockSpec((seq, NA_HEAD_DIM), lambda h, b, ks, sl: (b, heads + h)),
                  pl.BlockSpec((seq, NA_HEAD_DIM), lambda h, b, ks, sl: (b, 2 * heads + h)),
                  pl.BlockSpec((1, n_slabs, gq, gk), lambda h, b, ks, sl: (h, 0, 0, 0))],
        out_specs=pl.BlockSpec((seq, NA_HEAD_DIM), lambda h, b, ks, sl: (b, h)),
    )
    return pl.pallas_call(
        kern,
        grid_spec=grid_spec,
        out_shape=jax.ShapeDtypeStruct((n, d), BF16),
        compiler_params=_params("parallel", "arbitrary"),
        name="natten",
    )(jnp.asarray(k_start), jnp.asarray(slab_of), qkv, qkv, qkv, bias)


def _router_kernel(x_ref, g_ref, whi_ref, wlo_ref, b_ref, hp_ref, route_ref, *, d):
    x = x_ref[...]
    ms = jnp.mean(x * x, axis=-1, keepdims=True)
    hn = x * lax.rsqrt(ms + RMS_EPS) * g_ref[...]
    hb = hn.astype(BF16)
    hbf = hb.astype(F32)
    hlo = (hn - hbf).astype(BF16)
    logits = (jnp.dot(hb, whi_ref[...], preferred_element_type=F32)
              + (jnp.dot(hb, wlo_ref[...], preferred_element_type=F32)
                 + jnp.dot(hlo, whi_ref[...], preferred_element_type=F32))) + b_ref[...]

    lane = lax.broadcasted_iota(jnp.int32, logits.shape, 1)
    big = jnp.int32(ROUTE_LANES)
    is_grp = lane < N_GROUPS
    gl = jnp.where(is_grp, logits, NEG_INF)
    gmax = jnp.max(gl, axis=-1, keepdims=True)
    gsum = jnp.sum(jnp.where(is_grp, jnp.exp(gl - gmax), 0.0), axis=-1, keepdims=True)
    p_top = 1.0 / gsum
    g_top = jnp.min(jnp.where(is_grp & (gl == gmax), lane, big), axis=-1, keepdims=True)
    lo = N_GROUPS + g_top * EXPERTS_PER_GROUP
    in_grp = (lane >= lo) & (lane < lo + EXPERTS_PER_GROUP)
    sl = jnp.where(in_grp, logits, NEG_INF)
    v1 = jnp.max(sl, axis=-1, keepdims=True)
    i1 = jnp.min(jnp.where(in_grp & (sl == v1), lane, big), axis=-1, keepdims=True)
    rest = in_grp & (lane != i1)
    sl2 = jnp.where(rest, logits, NEG_INF)
    v2 = jnp.max(sl2, axis=-1, keepdims=True)
    i2 = jnp.min(jnp.where(rest & (sl2 == v2), lane, big), axis=-1, keepdims=True)
    ex = jnp.exp(v2 - v1)
    den = 1.0 + ex
    gate1 = p_top * (1.0 / den)
    gate2 = p_top * (ex / den)
    e1 = (i1 - N_GROUPS).astype(F32)
    e2 = (i2 - N_GROUPS).astype(F32)
    route_ref[...] = jnp.where(lane == 0, e1,
                               jnp.where(lane == 1, e2,
                                         jnp.where(lane == 2, gate1,
                                                   jnp.where(lane == 3, gate2, 0.0))))

    bits = lax.bitcast_convert_type(hbf, jnp.uint32)
    half = d // 2
    hp_ref[...] = bits[:, :half] | (bits[:, half:] >> 16)


def _router(x, g, w_hi, w_lo, b):
    n, d = x.shape
    tm = _tile(n, ROUTER_TM)
    kern = functools.partial(_router_kernel, d=d)
    vec = lambda i: (0, 0)
    return pl.pallas_call(
        kern,
        grid=(n // tm,),
        in_specs=[pl.BlockSpec((tm, d), lambda i: (i, 0)),
                  pl.BlockSpec((1, d), vec),
                  pl.BlockSpec((d, ROUTE_LANES), vec),
                  pl.BlockSpec((d, ROUTE_LANES), vec),
                  pl.BlockSpec((1, ROUTE_LANES), vec)],
        out_specs=[pl.BlockSpec((tm, d // 2), lambda i: (i, 0)),
                   pl.BlockSpec((tm, ROUTE_LANES), lambda i: (i, 0))],
        out_shape=[jax.ShapeDtypeStruct((n, d // 2), jnp.uint32),
                   jax.ShapeDtypeStruct((n, ROUTE_LANES), F32)],
        compiler_params=_params("parallel"),
        name="moe_router",
    )(x, g.reshape(1, d), w_hi, w_lo, b)


def _row_copy(src_ref, src_row, dst_ref, dst_row, sem):
    return pltpu.make_async_copy(src_ref.at[pl.ds(src_row, 1)], dst_ref.at[pl.ds(dst_row, 1)], sem)


def _pack_bf16_pair(hi, lo):
    hi_bits = lax.bitcast_convert_type(hi.astype(BF16).astype(F32), jnp.uint32)
    lo_bits = lax.bitcast_convert_type(lo.astype(BF16).astype(F32), jnp.uint32)
    return hi_bits | (lo_bits >> 16)


def _unpack_bf16_pair(w):
    return (lax.bitcast_convert_type(w & jnp.uint32(0xFFFF0000), F32),
            lax.bitcast_convert_type(w << 16, F32))


def _expert_kernel(ce_ref, nv_ref, nu_ref, tok_ref, hp_ref, wg_ref, wu_ref, wda_ref, wdb_ref, y_ref,
                   xraw_ref, xb_ref, h_ref, sem, *, rows, n_f, tf, n_chunks):
    c = pl.program_id(0)
    s = pl.program_id(1)
    sub, sub2 = EXPERT_SUB, 2 * EXPERT_SUB
    half = xraw_ref.shape[1]
    blocks_of = lambda k: (nv_ref[jnp.maximum(k, 0)] + sub - 1) // sub
    n_blk = jnp.where(c >= 1, blocks_of(c - 1), 0)
    n_blk_next = blocks_of(c)
    n_blk_wait = jnp.where(c >= 1, jnp.maximum(jnp.where(c >= 2, blocks_of(c - 2), 0), blocks_of(c - 1)), 0)
    rows_per_step = sub // n_f

    def gather_rows(i):
        r0 = i * sub + s * rows_per_step
        for j in range(rows_per_step):
            _row_copy(hp_ref, tok_ref[0, 0, r0 + j], xraw_ref, r0 + j, sem).start()

    @pl.when(s == 0)
    def _():
        def wait_block(i, carry):
            r0 = pl.multiple_of(i * sub, sub)
            pltpu.make_async_copy(hp_ref.at[pl.ds(0, sub)], xraw_ref.at[pl.ds(r0, sub)], sem).wait()
            return carry

        def unpack_block(i, carry):
            r0 = pl.multiple_of(i * sub, sub)
            hi, lo = _unpack_bf16_pair(xraw_ref[pl.ds(r0, sub), :])
            xb_ref[pl.ds(r0, sub), :half] = hi.astype(BF16)
            xb_ref[pl.ds(r0, sub), half:] = lo.astype(BF16)
            return carry

        lax.fori_loop(0, n_blk_wait, wait_block, 0)
        lax.fori_loop(0, n_blk, unpack_block, 0)

    @pl.when((s < n_f) & (n_blk > 0))
    def _():
        col = pl.multiple_of(s * tf, tf)

        def block(i, carry):
            r0 = pl.multiple_of(i * sub, sub)
            xb = xb_ref[pl.ds(r0, sub), :]
            g = jnp.dot(xb, wg_ref[0, 0].astype(BF16), preferred_element_type=F32)
            u = jnp.dot(xb, wu_ref[0, 0].astype(BF16), preferred_element_type=F32)
            act = (g * jax.nn.sigmoid(g) * u).astype(BF16)
            gather_rows(i)
            h_ref[pl.ds(r0, sub), pl.ds(col, tf)] = act
            return carry

        lax.fori_loop(0, n_blk, block, 0)

        @pl.when(n_blk % 2 == 1)
        def _():
            r0 = pl.multiple_of(n_blk * sub, sub)
            h_ref[pl.ds(r0, sub), pl.ds(col, tf)] = jnp.zeros((sub, tf), BF16)

    @pl.when(s < n_f)
    def _():
        def tail(i, carry):
            gather_rows(i)
            return carry

        lax.fori_loop(n_blk, jnp.maximum(n_blk, n_blk_next), tail, 0)

    @pl.when((c >= 1) & (c <= n_chunks) & (s >= n_f))
    def _():
        n_blk2 = (n_blk + 1) // 2

        def block(i, carry):
            r0 = pl.multiple_of(i * sub2, sub2)
            hb = h_ref[pl.ds(r0, sub2), :]
            ya = jnp.dot(hb, wda_ref[0, 0].astype(BF16), preferred_element_type=F32)
            yb = jnp.dot(hb, wdb_ref[0, 0].astype(BF16), preferred_element_type=F32)
            y_ref[pl.ds(r0, sub2), :] = _pack_bf16_pair(ya, yb)
            return carry

        def zero_block(i, carry):
            r0 = pl.multiple_of(i * sub2, sub2)
            y_ref[pl.ds(r0, sub2), :] = jnp.zeros((sub2, y_ref.shape[1]), jnp.uint32)
            return carry

        lax.fori_loop(0, n_blk2, block, 0)
        lax.fori_loop(n_blk2, rows // sub2, zero_block, 0)


def _experts(hp, tok_chunks, chunk_expert, chunk_valid, n_used, w_gate, w_up, w_down, layer):
    n, half = hp.shape
    d = 2 * half
    ff = w_gate.shape[-1]
    n_max, rows = tok_chunks.shape
    tf, tn = _tile(ff, EXPERT_TF), _tile(half, EXPERT_TN)
    n_f, n_n = ff // tf, half // tn
    assert rows % (2 * EXPERT_SUB) == 0 and EXPERT_SUB % n_f == 0 and chunk_valid.shape == (n_max + 2,)

    def chunk(c, nu):
        return jnp.clip(c - 1, 0, nu[0] - 1)

    def active(c, nu):
        return (c >= 1) & (c - 1 < nu[0])

    def gate_map(c, s, ce, nv, nu):
        nxt = jnp.clip(c, 0, nu[0] - 1)
        in_p1 = active(c, nu) & (s < n_f)
        has_next = c < nu[0]
        e = jnp.where(in_p1, ce[chunk(c, nu)], jnp.where(has_next, ce[nxt], ce[chunk(c, nu)]))
        t = jnp.where(in_p1, s, jnp.where(has_next, 0, n_f - 1))
        return (layer, e, 0, t)

    def down_tile(c, s, nu):
        return jnp.where(active(c, nu), jnp.maximum(s - n_f, 0), jnp.where(c == 0, 0, n_n - 1))

    def down_map_a(c, s, ce, nv, nu):
        return (layer, ce[chunk(c, nu)], 0, down_tile(c, s, nu))

    def down_map_b(c, s, ce, nv, nu):
        return (layer, ce[chunk(c, nu)], 0, n_n + down_tile(c, s, nu))

    def out_map(c, s, ce, nv, nu):
        t = jnp.where(c == 0, 0, jnp.where(c > n_max, n_n - 1, jnp.maximum(s - n_f, 0)))
        return (jnp.clip(c - 1, 0, n_max - 1), t)

    kern = functools.partial(_expert_kernel, rows=rows, n_f=n_f, tf=tf, n_chunks=n_max)
    grid_spec = pltpu.PrefetchScalarGridSpec(
        num_scalar_prefetch=3,
        grid=(n_max + 2, n_f + n_n),
        in_specs=[pl.BlockSpec((1, 1, rows), lambda c, s, ce, nv, nu: (jnp.minimum(c, n_max - 1), 0, 0),
                               memory_space=pltpu.SMEM),
                  pl.BlockSpec(memory_space=pl.ANY),
                  pl.BlockSpec((1, 1, d, tf), gate_map),
                  pl.BlockSpec((1, 1, d, tf), gate_map),
                  pl.BlockSpec((1, 1, ff, tn), down_map_a),
                  pl.BlockSpec((1, 1, ff, tn), down_map_b)],
        out_specs=pl.BlockSpec((rows, tn), out_map),
        scratch_shapes=[pltpu.VMEM((rows, half), jnp.uint32),
                        pltpu.VMEM((rows, d), BF16),
                        pltpu.VMEM((rows, ff), BF16),
                        pltpu.SemaphoreType.DMA(())],
    )
    return pl.pallas_call(
        kern,
        grid_spec=grid_spec,
        out_shape=jax.ShapeDtypeStruct((n_max * rows, half), jnp.uint32),
        compiler_params=_params("arbitrary", "arbitrary"),
        name="moe_experts",
    )(chunk_expert, chunk_valid, n_used.reshape(1), tok_chunks.reshape(n_max, 1, rows), hp, w_gate, w_up,
      w_down, w_down)


def _combine_kernel(pos_ref, posn_ref, x_ref, route_ref, gn_ref, y_ref, o0_ref, o1_ref, ybuf_ref, sems,
                    *, tc, nb, nb_first, last):
    i = pl.program_id(0)
    slot = i & 1

    def issue(p_ref, to_slot):
        def body(a, carry):
            dst = (a & 1) * tc + (a >> 1)
            pltpu.make_async_copy(y_ref.at[pl.ds(p_ref[0, 0, a], 1)],
                                  ybuf_ref.at[to_slot, pl.ds(dst, 1)], sems.at[to_slot]).start()
            return carry

        lax.fori_loop(0, 2 * tc, body, 0, unroll=8)

    @pl.when(i == 0)
    def _():
        issue(pos_ref, 0)

    @pl.when(i + 1 < nb)
    def _():
        issue(posn_ref, 1 - slot)

    pltpu.make_async_copy(y_ref.at[pl.ds(0, 2 * tc)], ybuf_ref.at[slot], sems.at[slot]).wait()

    route = route_ref[...]
    yb = ybuf_ref.at[slot]
    half = yb.shape[1]
    hi0, lo0 = _unpack_bf16_pair(yb[0:tc, :])
    hi1, lo1 = _unpack_bf16_pair(yb[tc:2 * tc, :])
    g0, g1 = route[:, 2:3], route[:, 3:4]
    x = jnp.concatenate([x_ref[:, :half] + (g0 * hi0 + g1 * hi1),
                         x_ref[:, half:] + (g0 * lo0 + g1 * lo1)], axis=1)
    ms = jnp.mean(x * x, axis=-1, keepdims=True)
    hn = x * lax.rsqrt(ms + RMS_EPS) * gn_ref[...]
    if last:
        @pl.when(i < nb_first)
        def _():
            o0_ref[...] = hn

        @pl.when(i >= nb_first)
        def _():
            o1_ref[...] = hn
    else:
        o0_ref[...] = x
        o1_ref[...] = hn.astype(o1_ref.dtype)


def _combine(x, y, pos, route, g_next, *, n_first, last):
    n, d = x.shape
    tc = _tile(n, COMBINE_TC)
    nb = n // tc
    assert n_first % tc == 0
    nb_first = n_first // tc
    kern = functools.partial(_combine_kernel, tc=tc, nb=nb, nb_first=nb_first, last=last)
    row = pl.BlockSpec((tc, d), lambda i: (i, 0))
    if last:
        out_specs = [pl.BlockSpec((tc, d), lambda i: (jnp.minimum(i, nb_first - 1), 0)),
                     pl.BlockSpec((tc, d), lambda i: (jnp.maximum(i - nb_first, 0), 0))]
        out_shape = [jax.ShapeDtypeStruct((n_first, d), F32), jax.ShapeDtypeStruct((n - n_first, d), F32)]
    else:
        out_specs = [row, row]
        out_shape = [jax.ShapeDtypeStruct((n, d), F32), jax.ShapeDtypeStruct((n, d), BF16)]
    pos3 = pos.reshape(nb, 1, 2 * tc)
    return pl.pallas_call(
        kern,
        grid=(nb,),
        in_specs=[pl.BlockSpec((1, 1, 2 * tc), lambda i: (i, 0, 0), memory_space=pltpu.SMEM),
                  pl.BlockSpec((1, 1, 2 * tc), lambda i: (jnp.minimum(i + 1, nb - 1), 0, 0),
                               memory_space=pltpu.SMEM),
                  row,
                  pl.BlockSpec((tc, ROUTE_LANES), lambda i: (i, 0)),
                  pl.BlockSpec((1, d), lambda i: (0, 0)),
                  pl.BlockSpec(memory_space=pl.ANY)],
        out_specs=out_specs,
        out_shape=out_shape,
        scratch_shapes=[pltpu.VMEM((2, 2 * tc, d // 2), jnp.uint32), pltpu.SemaphoreType.DMA((2,))],
        compiler_params=_params("arbitrary"),
        name="moe_combine",
    )(pos3, pos3, x, route, g_next.reshape(1, d), y)


def _dispatch_tables(route, rows):
    n = route.shape[0]
    a = n * TOP_K_INNER
    n_max = a // rows + N_EXPERTS
    e_flat = route[:, :TOP_K_INNER].astype(jnp.int32).reshape(a)
    order = jnp.argsort(e_flat, stable=True).astype(jnp.int32)
    sorted_pos = jnp.argsort(order).astype(jnp.int32)
    counts = jnp.sum((e_flat[:, None] == jnp.arange(N_EXPERTS, dtype=jnp.int32)[None, :]).astype(jnp.int32), axis=0)
    seg_start = jnp.cumsum(counts) - counts
    n_chunks = (counts + rows - 1) // rows
    chunk_end = jnp.cumsum(n_chunks)
    chunk_base = chunk_end - n_chunks
    n_used = chunk_end[-1].astype(jnp.int32)
    cidx = jnp.arange(n_max, dtype=jnp.int32)
    ce = jnp.minimum(jnp.searchsorted(chunk_end, cidx, side='right'), N_EXPERTS - 1).astype(jnp.int32)
    ce = jnp.where(cidx < n_used, ce, ce[n_used - 1])
    ustart = seg_start[ce] + (cidx - chunk_base[ce]) * rows
    tok_sorted = jnp.concatenate([order // TOP_K_INNER, jnp.zeros((rows,), jnp.int32)])
    tok_chunks = jax.vmap(lambda s0: lax.dynamic_slice(tok_sorted, (s0,), (rows,)))(jnp.minimum(ustart, a))
    pos = chunk_base[e_flat] * rows + (sorted_pos - seg_start[e_flat])
    valid = jnp.clip(counts[ce] - (cidx - chunk_base[ce]) * rows, 0, rows)
    valid = jnp.concatenate([jnp.where(cidx < n_used, valid, 0), jnp.zeros((2,), jnp.int32)]).astype(jnp.int32)
    return tok_chunks.astype(jnp.int32), ce, valid, n_used, pos.astype(jnp.int32)


def _moe(x, layer, ffn_norm, moe_w_grp, moe_b_grp, moe_w_sub, moe_b_sub, moe_w_gate, moe_w_up, moe_w_down,
         g_next, *, n_first, last):
    n, d = x.shape
    w_r = jnp.concatenate([moe_w_grp[layer],
                           jnp.transpose(moe_w_sub[layer], (1, 0, 2)).reshape(d, N_EXPERTS)], axis=1)
    w_r = jnp.pad(w_r, ((0, 0), (0, ROUTE_LANES - w_r.shape[1])))
    w_hi = w_r.astype(BF16)
    w_lo = (w_r - w_hi.astype(F32)).astype(BF16)
    b_r = jnp.concatenate([moe_b_grp[layer], moe_b_sub[layer].reshape(N_EXPERTS)])
    b_r = jnp.pad(b_r, (0, ROUTE_LANES - b_r.shape[0])).reshape(1, ROUTE_LANES).astype(F32)

    hp, route = _router(x, ffn_norm[layer], w_hi, w_lo, b_r)
    rows = _tile(n * TOP_K_INNER, EXPERT_ROWS)
    tok_chunks, ce, valid, n_used, pos = _dispatch_tables(route, rows)
    y = _experts(hp, tok_chunks, ce, valid, n_used, moe_w_gate, moe_w_up, moe_w_down, layer)
    return _combine(x, y, pos, route, g_next, n_first=n_first, last=last)


def kernel(x_prompt, x_sample, mix_norm, ffn_norm, final_norm, cv_w_pw1, cv_b_pw1, cv_w_dw, cv_b_dw, cv_ln_g, cv_ln_b, cv_w_pw2, cv_b_pw2, na_w_qkv, na_rpb, na_w_o, moe_w_grp, moe_b_grp, moe_w_sub, moe_b_sub, moe_w_gate, moe_w_up, moe_w_down):
    bp, seq, d = x_prompt.shape
    bs = x_sample.shape[0]
    assert x_sample.shape[1:] == (seq, d) and seq % GRID_W == 0
    n_p = bp * seq
    xp, xs = x_prompt.reshape(n_p, d), x_sample.reshape(bs * seq, d)
    moe_p = (ffn_norm, moe_w_grp, moe_b_grp, moe_w_sub, moe_b_sub, moe_w_gate, moe_w_up, moe_w_down)

    h = _rmsnorm_stacked(xp, xs, mix_norm[0], BF16)
    u = _glu_matmul(h, cv_w_pw1[0].astype(BF16), cv_b_pw1[0])
    v = _conv_ln_swish(u, cv_w_dw[0], cv_b_dw[0], cv_ln_g[0], cv_ln_b[0], seq)
    x = _matmul_bias_res_stacked(v, cv_w_pw2[0].astype(BF16), cv_b_pw2[0], xp, xs, name="pw2_res")
    x, h = _moe(x, 0, *moe_p, mix_norm[1], n_first=n_p, last=False)

    qkv = _matmul(h, na_w_qkv[0].astype(BF16), out_dtype=BF16, name="qkv")
    o = _natten(qkv, na_rpb[0], seq)
    x = _matmul(o, na_w_o[0].astype(BF16), res=x, name="wo_res")
    yp, ys = _moe(x, 1, *moe_p, final_norm, n_first=n_p, last=True)
    return (yp.reshape(bp, seq, d), ys.reshape(bs, seq, d))
```

```python
import functools

import numpy as np
import jax
import jax.numpy as jnp
from jax import lax
from jax.experimental import pallas as pl
from jax.experimental.pallas import tpu as pltpu

GRID_W = 64
CONV_WIDTH = 31
NA_HEAD_DIM = 128
WIN_H = 8
WIN_W = 16
N_GROUPS = 8
EXPERTS_PER_GROUP = 8
N_EXPERTS = N_GROUPS * EXPERTS_PER_GROUP
TOP_K_INNER = 2
RMS_EPS = 1e-6
LN_EPS = 1e-5
NEG_INF = -1e30
LOG2_E = 1.4426950408889634

LANES = 128
BF16_SUBLANES = 16
V7X_VMEM_BUDGET = 58 * 1024 * 1024

MM_TM = 1024
MM_TN = 512
MM_TN_WIDE = 1024
NORM_TM = 256
CONV_TS = 256
CONV_HALO = BF16_SUBLANES
CONV_RB = 64
LN_RB = 64
NA_QROWS = 4
NA_KROWS = NA_QROWS + WIN_H
NA_UNROLL = 4
ROUTER_TM = 256
ROUTE_LANES = LANES
EXPERT_ROWS = 1024
EXPERT_SUB = 256
EXPERT_TF = 256
EXPERT_TN = 512
COMBINE_TC = 256

F32 = jnp.float32
BF16 = jnp.bfloat16


def _tile(n, pref):
    t = min(n, pref)
    assert n % t == 0, (n, pref)
    return t


def _params(*sem):
    return pltpu.CompilerParams(dimension_semantics=sem, vmem_limit_bytes=V7X_VMEM_BUDGET)


def _first_blocks(nb_first):
    return (lambda i: (jnp.minimum(i, nb_first - 1), 0)), (lambda i: (jnp.maximum(i - nb_first, 0), 0))


def _rmsnorm_kernel(xa_ref, xb_ref, g_ref, o_ref, *, nb_first):
    def norm(x_ref):
        x = x_ref[...]
        ms = jnp.mean(x * x, axis=-1, keepdims=True)
        o_ref[...] = (x * lax.rsqrt(ms + RMS_EPS) * g_ref[...]).astype(o_ref.dtype)

    i = pl.program_id(0)
    pl.when(i < nb_first)(lambda: norm(xa_ref))
    pl.when(i >= nb_first)(lambda: norm(xb_ref))


def _rmsnorm_stacked(xa, xb, g, out_dtype):
    na, d = xa.shape
    n = na + xb.shape[0]
    tm = _tile(na, NORM_TM)
    assert xb.shape[0] % tm == 0
    a_map, b_map = _first_blocks(na // tm)
    return pl.pallas_call(
        functools.partial(_rmsnorm_kernel, nb_first=na // tm),
        grid=(n // tm,),
        in_specs=[pl.BlockSpec((tm, d), a_map),
                  pl.BlockSpec((tm, d), b_map),
                  pl.BlockSpec((1, d), lambda i: (0, 0))],
        out_specs=pl.BlockSpec((tm, d), lambda i: (i, 0)),
        out_shape=jax.ShapeDtypeStruct((n, d), out_dtype),
        compiler_params=_params("arbitrary"),
        name="rmsnorm",
    )(xa, xb, g.reshape(1, d))


def _glu_kernel(h_ref, wa_ref, wg_ref, ba_ref, bg_ref, o_ref):
    h = h_ref[...]
    a = jnp.dot(h, wa_ref[...], preferred_element_type=F32) + ba_ref[...]
    g = jnp.dot(h, wg_ref[...], preferred_element_type=F32) + bg_ref[...]
    o_ref[...] = (a * jax.nn.sigmoid(g)).astype(o_ref.dtype)


def _glu_matmul(h, w, b):
    n, k = h.shape
    d = w.shape[1] // 2
    tm, tn = _tile(n, MM_TM), _tile(d, MM_TN)
    nj = d // tn
    b2 = b.reshape(1, 2 * d)
    return pl.pallas_call(
        _glu_kernel,
        grid=(n // tm, nj),
        in_specs=[pl.BlockSpec((tm, k), lambda i, j: (i, 0)),
                  pl.BlockSpec((k, tn), lambda i, j: (0, j)),
                  pl.BlockSpec((k, tn), lambda i, j: (0, j + nj)),
                  pl.BlockSpec((1, tn), lambda i, j: (0, j)),
                  pl.BlockSpec((1, tn), lambda i, j: (0, j + nj))],
        out_specs=pl.BlockSpec((tm, tn), lambda i, j: (i, j)),
        out_shape=jax.ShapeDtypeStruct((n, d), BF16),
        compiler_params=_params("parallel", "arbitrary"),
        name="pw1_glu",
    )(h, w, w, b2, b2)


def _mm_kernel(a_ref, w_ref, o_ref):
    o_ref[...] = jnp.dot(a_ref[...], w_ref[...], preferred_element_type=F32).astype(o_ref.dtype)


def _mm_bias_res2_kernel(a_ref, w_ref, b_ref, ra_ref, rb_ref, o_ref, *, nb_first):
    y = jnp.dot(a_ref[...], w_ref[...], preferred_element_type=F32) + b_ref[...]
    i = pl.program_id(0)

    @pl.when(i < nb_first)
    def _():
        o_ref[...] = ra_ref[...] + y

    @pl.when(i >= nb_first)
    def _():
        o_ref[...] = rb_ref[...] + y


def _matmul_bias_res_stacked(a, w, bias, res_a, res_b, *, name):
    n, k = a.shape
    m = w.shape[1]
    tm, tn = _tile(res_a.shape[0], MM_TM), _tile(m, MM_TN)
    assert res_a.shape[0] + res_b.shape[0] == n and res_b.shape[0] % tm == 0
    nbf, nj = res_a.shape[0] // tm, m // tn
    o_spec = pl.BlockSpec((tm, tn), lambda i, j: (i, j))
    ra_spec = pl.BlockSpec((tm, tn), lambda i, j: (jnp.minimum(i, nbf - 1), jnp.where(i < nbf, j, nj - 1)))
    rb_spec = pl.BlockSpec((tm, tn), lambda i, j: (jnp.maximum(i - nbf, 0), jnp.where(i >= nbf, j, 0)))
    return pl.pallas_call(
        functools.partial(_mm_bias_res2_kernel, nb_first=nbf),
        grid=(n // tm, nj),
        in_specs=[pl.BlockSpec((tm, k), lambda i, j: (i, 0)),
                  pl.BlockSpec((k, tn), lambda i, j: (0, j)),
                  pl.BlockSpec((1, tn), lambda i, j: (0, j)),
                  ra_spec, rb_spec],
        out_specs=o_spec,
        out_shape=jax.ShapeDtypeStruct((n, m), F32),
        compiler_params=_params("arbitrary", "arbitrary"),
        name=name,
    )(a, w, bias.reshape(1, m), res_a, res_b)


def _mm_res_kernel(a_ref, w_ref, r_ref, o_ref):
    o_ref[...] = r_ref[...] + jnp.dot(a_ref[...], w_ref[...], preferred_element_type=F32)


def _matmul(a, w, *, res=None, out_dtype=F32, name="matmul"):
    n, k = a.shape
    m = w.shape[1]
    tm, tn = _tile(n, MM_TM), _tile(m, MM_TN_WIDE)
    a_spec = pl.BlockSpec((tm, k), lambda i, j: (i, 0))
    w_spec = pl.BlockSpec((k, tn), lambda i, j: (0, j))
    o_spec = pl.BlockSpec((tm, tn), lambda i, j: (i, j))
    if res is None:
        kern, specs, args = _mm_kernel, [a_spec, w_spec], (a, w)
    else:
        kern, specs, args = _mm_res_kernel, [a_spec, w_spec, o_spec], (a, w, res)
    return pl.pallas_call(
        kern,
        grid=(n // tm, m // tn),
        in_specs=specs,
        out_specs=o_spec,
        out_shape=jax.ShapeDtypeStruct((n, m), out_dtype),
        compiler_params=_params("parallel", "arbitrary"),
        name=name,
    )(*args)


def _conv_kernel(prev_ref, cur_ref, next_ref, w_ref, bdw_ref, g_ref, b_ref, o_ref,
                 win_ref, acc_ref, *, ts, n_tblk, d):
    i = pl.program_id(1)
    halo = CONV_HALO
    n_cb = d // LANES
    has_prev = i > 0
    has_next = i < n_tblk - 1
    for cb in range(n_cb):
        sl = slice(cb * LANES, (cb + 1) * LANES)
        win_ref[cb, 0:halo, :] = jnp.where(has_prev, prev_ref[:, sl].astype(F32), 0.0)
        win_ref[cb, halo:halo + ts, :] = cur_ref[:, sl].astype(F32)
        win_ref[cb, halo + ts:halo + ts + halo, :] = jnp.where(has_next, next_ref[:, sl].astype(F32), 0.0)

    base = halo - CONV_WIDTH // 2
    span = 2 * CONV_RB
    starts = [sp * span + par for sp in range(ts // span) for par in (0, 1)]

    def col_body(cb, carry):
        accs = [jnp.zeros((CONV_RB, LANES), F32) for _ in starts]
        for k in range(CONV_WIDTH):
            wk = w_ref[cb, k:k + 1, :]
            for j, st in enumerate(starts):
                accs[j] = accs[j] + win_ref[cb, pl.ds(st + base + k, CONV_RB, stride=2), :] * wk
        bias = bdw_ref[cb]
        for j, st in enumerate(starts):
            acc_ref[cb, pl.ds(st, CONV_RB, stride=2), :] = accs[j] + bias
        return carry

    lax.fori_loop(0, n_cb, col_body, 0)

    inv_d = 1.0 / d

    def ln_body(rb, carry):
        r0 = pl.multiple_of(rb * LN_RB, LN_RB)
        y = acc_ref[:, pl.ds(r0, LN_RB), :]
        mu = jnp.sum(jnp.sum(y, axis=0), axis=-1, keepdims=True) * inv_d
        yc = y - mu[None]
        var = jnp.sum(jnp.sum(yc * yc, axis=0), axis=-1, keepdims=True) * inv_d
        z = yc * lax.rsqrt(var + LN_EPS)[None] * g_ref[...] + b_ref[...]
        z = (z * jax.nn.sigmoid(z)).astype(o_ref.dtype)
        for cb in range(n_cb):
            o_ref[pl.ds(r0, LN_RB), cb * LANES:(cb + 1) * LANES] = z[cb]
        return carry

    lax.fori_loop(0, ts // LN_RB, ln_body, 0)


def _conv_ln_swish(u, w_dw, b_dw, ln_g, ln_b, seq):
    n, d = u.shape
    n_seq = n // seq
    ts = _tile(seq, CONV_TS)
    n_tblk = seq // ts
    halo = CONV_HALO
    n_cb = d // LANES
    assert halo >= CONV_WIDTH // 2 and ts % halo == 0 and ts % (2 * CONV_RB) == 0 and d % LANES == 0
    hb = ts // halo
    last_hblk = n // halo - 1

    def prev_map(b, i):
        return (jnp.maximum((b * n_tblk + i) * hb - 1, 0), 0)

    def next_map(b, i):
        return (jnp.minimum((b * n_tblk + i + 1) * hb, last_hblk), 0)

    slab = lambda v: v.reshape(-1, n_cb, LANES).transpose(1, 0, 2)
    vec = lambda b, i: (0, 0, 0)
    kern = functools.partial(_conv_kernel, ts=ts, n_tblk=n_tblk, d=d)
    return pl.pallas_call(
        kern,
        grid=(n_seq, n_tblk),
        in_specs=[pl.BlockSpec((halo, d), prev_map),
                  pl.BlockSpec((ts, d), lambda b, i: (b * n_tblk + i, 0)),
                  pl.BlockSpec((halo, d), next_map),
                  pl.BlockSpec((n_cb, CONV_WIDTH, LANES), vec),
                  pl.BlockSpec((n_cb, 1, LANES), vec),
                  pl.BlockSpec((n_cb, 1, LANES), vec),
                  pl.BlockSpec((n_cb, 1, LANES), vec)],
        out_specs=pl.BlockSpec((ts, d), lambda b, i: (b * n_tblk + i, 0)),
        out_shape=jax.ShapeDtypeStruct((n, d), BF16),
        scratch_shapes=[pltpu.VMEM((n_cb, ts + 2 * halo, LANES), F32),
                        pltpu.VMEM((n_cb, ts, LANES), F32)],
        compiler_params=_params("parallel", "arbitrary"),
        name="dwconv_ln_swish",
    )(u, u, u, slab(w_dw), slab(b_dw), slab(ln_g), slab(ln_b))


def _natten_plan(rows):
    kh = min(WIN_H, rows)
    assert rows % NA_QROWS == 0 and rows >= NA_KROWS
    n_groups = rows // NA_QROWS
    k_start = np.clip(np.arange(n_groups) * NA_QROWS - kh // 2, 0, rows - NA_KROWS)
    patterns, slab_of = [], []
    for g in range(n_groups):
        pat = np.full((NA_QROWS, NA_KROWS), -1, np.int64)
        for qi in range(NA_QROWS):
            r = g * NA_QROWS + qi
            rs = int(np.clip(r - kh // 2, 0, rows - kh))
            for j in range(NA_KROWS):
                kr = int(k_start[g]) + j
                if rs <= kr < rs + kh:
                    pat[qi, j] = kr - r + WIN_H - 1
            assert (pat[qi] >= 0).sum() == kh
        key = pat.tobytes()
        if key not in [p.tobytes() for p in patterns]:
            patterns.append(pat)
        slab_of.append([p.tobytes() for p in patterns].index(key))
    return k_start.astype(np.int32), np.asarray(slab_of, np.int32), np.stack(patterns)


def _natten_bias(rpb, patterns):
    w = GRID_W
    qc = np.arange(w)
    cs = np.clip(qc - WIN_W // 2, 0, w - WIN_W)
    kc = np.arange(w)
    col_mask = (kc[None, :] >= cs[:, None]) & (kc[None, :] < cs[:, None] + WIN_W)
    col_idx = np.clip(kc[None, :] - qc[:, None] + WIN_W - 1, 0, 2 * WIN_W - 2)
    heads = rpb.shape[0]
    blocks = jnp.where(jnp.asarray(col_mask)[None, None], rpb.astype(F32)[:, :, col_idx] * LOG2_E, NEG_INF)
    masked = jnp.full((heads, w, w), NEG_INF, F32)
    slabs = []
    for pat in patterns:
        rows_ = [jnp.concatenate([blocks[:, int(ri)] if ri >= 0 else masked for ri in pat_q], axis=-1)
                 for pat_q in pat]
        slabs.append(jnp.concatenate(rows_, axis=-2))
    return jnp.stack(slabs, axis=1)


def _natten_kernel(kstart_ref, slab_ref, q_ref, k_ref, v_ref, bias_ref, o_ref, *, n_groups, scale):
    gq = NA_QROWS * GRID_W
    gk = NA_KROWS * GRID_W

    def body(g, carry):
        q0 = pl.multiple_of(g * gq, gq)
        k0 = pl.multiple_of(kstart_ref[g] * GRID_W, GRID_W)
        q = q_ref[pl.ds(q0, gq), :]
        kb = k_ref[pl.ds(k0, gk), :]
        vb = v_ref[pl.ds(k0, gk), :]
        s = lax.dot_general(q, kb, (((1,), (1,)), ((), ())), preferred_element_type=F32) * (scale * LOG2_E)
        s = s + bias_ref[0, slab_ref[g]]
        m = jnp.max(s, axis=-1, keepdims=True)
        e = jnp.exp2(s - m)
        inv = 1.0 / jnp.sum(e, axis=-1, keepdims=True)
        o = jnp.dot(e.astype(BF16), vb, preferred_element_type=F32) * inv
        o_ref[pl.ds(q0, gq), :] = o.astype(o_ref.dtype)
        return carry

    lax.fori_loop(0, n_groups, body, 0, unroll=NA_UNROLL)


def _natten(qkv, rpb, seq):
    n, d3 = qkv.shape
    d = d3 // 3
    heads = d // NA_HEAD_DIM
    n_seq = n // seq
    rows = seq // GRID_W
    k_start, slab_of, patterns = _natten_plan(rows)
    bias = _natten_bias(rpb, patterns)
    n_slabs = patterns.shape[0]
    gq, gk = NA_QROWS * GRID_W, NA_KROWS * GRID_W
    kern = functools.partial(_natten_kernel, n_groups=rows // NA_QROWS, scale=NA_HEAD_DIM ** -0.5)
    grid_spec = pltpu.PrefetchScalarGridSpec(
        num_scalar_prefetch=2,
        grid=(heads, n_seq),
        in_specs=[pl.BlockSpec((seq, NA_HEAD_DIM), lambda h, b, ks, sl: (b, h)),
                  pl.BlockSpec((seq, NA_HEAD_DIM), lambda h, b, ks, sl: (b, heads + h)),
                  pl.BlockSpec((seq, NA_HEAD_DIM), lambda h, b, ks, sl: (b, 2 * heads + h)),
                  pl.BlockSpec((1, n_slabs, gq, gk), lambda h, b, ks, sl: (h, 0, 0, 0))],
        out_specs=pl.BlockSpec((seq, NA_HEAD_DIM), lambda h, b, ks, sl: (b, h)),
    )
    return pl.pallas_call(
        kern,
        grid_spec=grid_spec,
        out_shape=jax.ShapeDtypeStruct((n, d), BF16),
        compiler_params=_params("parallel", "arbitrary"),
        name="natten",
    )(jnp.asarray(k_start), jnp.asarray(slab_of), qkv, qkv, qkv, bias)


def _router_kernel(x_ref, g_ref, whi_ref, wlo_ref, b_ref, hp_ref, route_ref, *, d):
    x = x_ref[...]
    ms = jnp.mean(x * x, axis=-1, keepdims=True)
    hn = x * lax.rsqrt(ms + RMS_EPS) * g_ref[...]
    hb = hn.astype(BF16)
    hbf = hb.astype(F32)
    hlo = (hn - hbf).astype(BF16)
    logits = (jnp.dot(hb, whi_ref[...], preferred_element_type=F32)
              + (jnp.dot(hb, wlo_ref[...], preferred_element_type=F32)
                 + jnp.dot(hlo, whi_ref[...], preferred_element_type=F32))) + b_ref[...]

    lane = lax.broadcasted_iota(jnp.int32, logits.shape, 1)
    big = jnp.int32(ROUTE_LANES)
    is_grp = lane < N_GROUPS
    gl = jnp.where(is_grp, logits, NEG_INF)
    gmax = jnp.max(gl, axis=-1, keepdims=True)
    gsum = jnp.sum(jnp.where(is_grp, jnp.exp(gl - gmax), 0.0), axis=-1, keepdims=True)
    p_top = 1.0 / gsum
    g_top = jnp.min(jnp.where(is_grp & (gl == gmax), lane, big), axis=-1, keepdims=True)
    lo = N_GROUPS + g_top * EXPERTS_PER_GROUP
    in_grp = (lane >= lo) & (lane < lo + EXPERTS_PER_GROUP)
    sl = jnp.where(in_grp, logits, NEG_INF)
    v1 = jnp.max(sl, axis=-1, keepdims=True)
    i1 = jnp.min(jnp.where(in_grp & (sl == v1), lane, big), axis=-1, keepdims=True)
    rest = in_grp & (lane != i1)
    sl2 = jnp.where(rest, logits, NEG_INF)
    v2 = jnp.max(sl2, axis=-1, keepdims=True)
    i2 = jnp.min(jnp.where(rest & (sl2 == v2), lane, big), axis=-1, keepdims=True)
    ex = jnp.exp(v2 - v1)
    den = 1.0 + ex
    gate1 = p_top * (1.0 / den)
    gate2 = p_top * (ex / den)
    e1 = (i1 - N_GROUPS).astype(F32)
    e2 = (i2 - N_GROUPS).astype(F32)
    route_ref[...] = jnp.where(lane == 0, e1,
                               jnp.where(lane == 1, e2,
                                         jnp.where(lane == 2, gate1,
                                                   jnp.where(lane == 3, gate2, 0.0))))

    bits = lax.bitcast_convert_type(hbf, jnp.uint32)
    half = d // 2
    hp_ref[...] = bits[:, :half] | (bits[:, half:] >> 16)


def _router(x, g, w_hi, w_lo, b):
    n, d = x.shape
    tm = _tile(n, ROUTER_TM)
    kern = functools.partial(_router_kernel, d=d)
    vec = lambda i: (0, 0)
    return pl.pallas_call(
        kern,
        grid=(n // tm,),
        in_specs=[pl.BlockSpec((tm, d), lambda i: (i, 0)),
                  pl.BlockSpec((1, d), vec),
                  pl.BlockSpec((d, ROUTE_LANES), vec),
                  pl.BlockSpec((d, ROUTE_LANES), vec),
                  pl.BlockSpec((1, ROUTE_LANES), vec)],
        out_specs=[pl.BlockSpec((tm, d // 2), lambda i: (i, 0)),
                   pl.BlockSpec((tm, ROUTE_LANES), lambda i: (i, 0))],
        out_shape=[jax.ShapeDtypeStruct((n, d // 2), jnp.uint32),
                   jax.ShapeDtypeStruct((n, ROUTE_LANES), F32)],
        compiler_params=_params("parallel"),
        name="moe_router",
    )(x, g.reshape(1, d), w_hi, w_lo, b)


def _row_copy(src_ref, src_row, dst_ref, dst_row, sem):
    return pltpu.make_async_copy(src_ref.at[pl.ds(src_row, 1)], dst_ref.at[pl.ds(dst_row, 1)], sem)


def _pack_bf16_pair(hi, lo):
    hi_bits = lax.bitcast_convert_type(hi.astype(BF16).astype(F32), jnp.uint32)
    lo_bits = lax.bitcast_convert_type(lo.astype(BF16).astype(F32), jnp.uint32)
    return hi_bits | (lo_bits >> 16)


def _unpack_bf16_pair(w):
    return (lax.bitcast_convert_type(w & jnp.uint32(0xFFFF0000), F32),
            lax.bitcast_convert_type(w << 16, F32))


def _expert_kernel(ce_ref, nv_ref, nu_ref, tok_ref, hp_ref, wg_ref, wu_ref, wda_ref, wdb_ref, y_ref,
                   xraw_ref, xb_ref, h_ref, sem, *, rows, n_f, tf, n_chunks):
    c = pl.program_id(0)
    s = pl.program_id(1)
    sub, sub2 = EXPERT_SUB, 2 * EXPERT_SUB
    half = xraw_ref.shape[1]
    blocks_of = lambda k: (nv_ref[jnp.maximum(k, 0)] + sub - 1) // sub
    n_blk = jnp.where(c >= 1, blocks_of(c - 1), 0)
    n_blk_next = blocks_of(c)
    n_blk_wait = jnp.where(c >= 1, jnp.maximum(jnp.where(c >= 2, blocks_of(c - 2), 0), blocks_of(c - 1)), 0)
    rows_per_step = sub // n_f

    def gather_rows(i):
        r0 = i * sub + s * rows_per_step
        for j in range(rows_per_step):
            _row_copy(hp_ref, tok_ref[0, 0, r0 + j], xraw_ref, r0 + j, sem).start()

    @pl.when(s == 0)
    def _():
        def wait_block(i, carry):
            r0 = pl.multiple_of(i * sub, sub)
            pltpu.make_async_copy(hp_ref.at[pl.ds(0, sub)], xraw_ref.at[pl.ds(r0, sub)], sem).wait()
            return carry

        def unpack_block(i, carry):
            r0 = pl.multiple_of(i * sub, sub)
            hi, lo = _unpack_bf16_pair(xraw_ref[pl.ds(r0, sub), :])
            xb_ref[pl.ds(r0, sub), :half] = hi.astype(BF16)
            xb_ref[pl.ds(r0, sub), half:] = lo.astype(BF16)
            return carry

        lax.fori_loop(0, n_blk_wait, wait_block, 0)
        lax.fori_loop(0, n_blk, unpack_block, 0)

    @pl.when((s < n_f) & (n_blk > 0))
    def _():
        col = pl.multiple_of(s * tf, tf)

        def block(i, carry):
            r0 = pl.multiple_of(i * sub, sub)
            xb = xb_ref[pl.ds(r0, sub), :]
            g = jnp.dot(xb, wg_ref[0, 0].astype(BF16), preferred_element_type=F32)
            u = jnp.dot(xb, wu_ref[0, 0].astype(BF16), preferred_element_type=F32)
            act = (g * jax.nn.sigmoid(g) * u).astype(BF16)
            gather_rows(i)
            h_ref[pl.ds(r0, sub), pl.ds(col, tf)] = act
            return carry

        lax.fori_loop(0, n_blk, block, 0)

        @pl.when(n_blk % 2 == 1)
        def _():
            r0 = pl.multiple_of(n_blk * sub, sub)
            h_ref[pl.ds(r0, sub), pl.ds(col, tf)] = jnp.zeros((sub, tf), BF16)

    @pl.when(s < n_f)
    def _():
        def tail(i, carry):
            gather_rows(i)
            return carry

        lax.fori_loop(n_blk, jnp.maximum(n_blk, n_blk_next), tail, 0)

    @pl.when((c >= 1) & (c <= n_chunks) & (s >= n_f))
    def _():
        n_blk2 = (n_blk + 1) // 2

        def block(i, carry):
            r0 = pl.multiple_of(i * sub2, sub2)
            hb = h_ref[pl.ds(r0, sub2), :]
            ya = jnp.dot(hb, wda_ref[0, 0].astype(BF16), preferred_element_type=F32)
            yb = jnp.dot(hb, wdb_ref[0, 0].astype(BF16), preferred_element_type=F32)
            y_ref[pl.ds(r0, sub2), :] = _pack_bf16_pair(ya, yb)
            return carry

        def zero_block(i, carry):
            r0 = pl.multiple_of(i * sub2, sub2)
            y_ref[pl.ds(r0, sub2), :] = jnp.zeros((sub2, y_ref.shape[1]), jnp.uint32)
            return carry

        lax.fori_loop(0, n_blk2, block, 0)
        lax.fori_loop(n_blk2, rows // sub2, zero_block, 0)


def _experts(hp, tok_chunks, chunk_expert, chunk_valid, n_used, w_gate, w_up, w_down, layer):
    n, half = hp.shape
    d = 2 * half
    ff = w_gate.shape[-1]
    n_max, rows = tok_chunks.shape
    tf, tn = _tile(ff, EXPERT_TF), _tile(half, EXPERT_TN)
    n_f, n_n = ff // tf, half // tn
    assert rows % (2 * EXPERT_SUB) == 0 and EXPERT_SUB % n_f == 0 and chunk_valid.shape == (n_max + 2,)

    def chunk(c, nu):
        return jnp.clip(c - 1, 0, nu[0] - 1)

    def active(c, nu):
        return (c >= 1) & (c - 1 < nu[0])

    def gate_map(c, s, ce, nv, nu):
        nxt = jnp.clip(c, 0, nu[0] - 1)
        in_p1 = active(c, nu) & (s < n_f)
        has_next = c < nu[0]
        e = jnp.where(in_p1, ce[chunk(c, nu)], jnp.where(has_next, ce[nxt], ce[chunk(c, nu)]))
        t = jnp.where(in_p1, s, jnp.where(has_next, 0, n_f - 1))
        return (layer, e, 0, t)

    def down_tile(c, s, nu):
        return jnp.where(active(c, nu), jnp.maximum(s - n_f, 0), jnp.where(c == 0, 0, n_n - 1))

    def down_map_a(c, s, ce, nv, nu):
        return (layer, ce[chunk(c, nu)], 0, down_tile(c, s, nu))

    def down_map_b(c, s, ce, nv, nu):
        return (layer, ce[chunk(c, nu)], 0, n_n + down_tile(c, s, nu))

    def out_map(c, s, ce, nv, nu):
        t = jnp.where(c == 0, 0, jnp.where(c > n_max, n_n - 1, jnp.maximum(s - n_f, 0)))
        return (jnp.clip(c - 1, 0, n_max - 1), t)

    kern = functools.partial(_expert_kernel, rows=rows, n_f=n_f, tf=tf, n_chunks=n_max)
    grid_spec = pltpu.PrefetchScalarGridSpec(
        num_scalar_prefetch=3,
        grid=(n_max + 2, n_f + n_n),
        in_specs=[pl.BlockSpec((1, 1, rows), lambda c, s, ce, nv, nu: (jnp.minimum(c, n_max - 1), 0, 0),
                               memory_space=pltpu.SMEM),
                  pl.BlockSpec(memory_space=pl.ANY),
                  pl.BlockSpec((1, 1, d, tf), gate_map),
                  pl.BlockSpec((1, 1, d, tf), gate_map),
                  pl.BlockSpec((1, 1, ff, tn), down_map_a),
                  pl.BlockSpec((1, 1, ff, tn), down_map_b)],
        out_specs=pl.BlockSpec((rows, tn), out_map),
        scratch_shapes=[pltpu.VMEM((rows, half), jnp.uint32),
                        pltpu.VMEM((rows, d), BF16),
                        pltpu.VMEM((rows, ff), BF16),
                        pltpu.SemaphoreType.DMA(())],
    )
    return pl.pallas_call(
        kern,
        grid_spec=grid_spec,
        out_shape=jax.ShapeDtypeStruct((n_max * rows, half), jnp.uint32),
        compiler_params=_params("arbitrary", "arbitrary"),
        name="moe_experts",
    )(chunk_expert, chunk_valid, n_used.reshape(1), tok_chunks.reshape(n_max, 1, rows), hp, w_gate, w_up,
      w_down, w_down)


def _combine_kernel(pos_ref, posn_ref, x_ref, route_ref, gn_ref, y_ref, o0_ref, o1_ref, ybuf_ref, sems,
                    *, tc, nb, nb_first, last):
    i = pl.program_id(0)
    slot = i & 1

    def issue(p_ref, to_slot):
        def body(t, carry):
            for k in range(TOP_K_INNER):
                pltpu.make_async_copy(y_ref.at[pl.ds(p_ref[0, 0, TOP_K_INNER * t + k], 1)],
                                      ybuf_ref.at[to_slot, pl.ds(k * tc + t, 1)],
                                      sems.at[to_slot]).start(priority=k)
            return carry

        lax.fori_loop(0, tc, body, 0, unroll=8)

    @pl.when(i == 0)
    def _():
        issue(pos_ref, 0)

    @pl.when(i + 1 < nb)
    def _():
        issue(posn_ref, 1 - slot)

    pltpu.make_async_copy(y_ref.at[pl.ds(0, 2 * tc)], ybuf_ref.at[slot], sems.at[slot]).wait()

    route = route_ref[...]
    yb = ybuf_ref.at[slot]
    half = yb.shape[1]
    hi0, lo0 = _unpack_bf16_pair(yb[0:tc, :])
    hi1, lo1 = _unpack_bf16_pair(yb[tc:2 * tc, :])
    g0, g1 = route[:, 2:3], route[:, 3:4]
    x = jnp.concatenate([x_ref[:, :half] + (g0 * hi0 + g1 * hi1),
                         x_ref[:, half:] + (g0 * lo0 + g1 * lo1)], axis=1)
    ms = jnp.mean(x * x, axis=-1, keepdims=True)
    hn = x * lax.rsqrt(ms + RMS_EPS) * gn_ref[...]
    if last:
        @pl.when(i < nb_first)
        def _():
            o0_ref[...] = hn

        @pl.when(i >= nb_first)
        def _():
            o1_ref[...] = hn
    else:
        o0_ref[...] = x
        o1_ref[...] = hn.astype(o1_ref.dtype)


def _combine(x, y, pos, route, g_next, *, n_first, last):
    n, d = x.shape
    tc = _tile(n, COMBINE_TC)
    nb = n // tc
    assert n_first % tc == 0
    nb_first = n_first // tc
    kern = functools.partial(_combine_kernel, tc=tc, nb=nb, nb_first=nb_first, last=last)
    row = pl.BlockSpec((tc, d), lambda i: (i, 0))
    if last:
        out_specs = [pl.BlockSpec((tc, d), lambda i: (jnp.minimum(i, nb_first - 1), 0)),
                     pl.BlockSpec((tc, d), lambda i: (jnp.maximum(i - nb_first, 0), 0))]
        out_shape = [jax.ShapeDtypeStruct((n_first, d), F32), jax.ShapeDtypeStruct((n - n_first, d), F32)]
    else:
        out_specs = [row, row]
        out_shape = [jax.ShapeDtypeStruct((n, d), F32), jax.ShapeDtypeStruct((n, d), BF16)]
    pos3 = pos.reshape(nb, 1, 2 * tc)
    return pl.pallas_call(
        kern,
        grid=(nb,),
        in_specs=[pl.BlockSpec((1, 1, 2 * tc), lambda i: (i, 0, 0), memory_space=pltpu.SMEM),
                  pl.BlockSpec((1, 1, 2 * tc), lambda i: (jnp.minimum(i + 1, nb - 1), 0, 0),
                               memory_space=pltpu.SMEM),
                  row,
                  pl.BlockSpec((tc, ROUTE_LANES), lambda i: (i, 0)),
                  pl.BlockSpec((1, d), lambda i: (0, 0)),
                  pl.BlockSpec(memory_space=pl.ANY)],
        out_specs=out_specs,
        out_shape=out_shape,
        scratch_shapes=[pltpu.VMEM((2, 2 * tc, d // 2), jnp.uint32), pltpu.SemaphoreType.DMA((2,))],
        compiler_params=_params("arbitrary"),
        name="moe_combine",
    )(pos3, pos3, x, route, g_next.reshape(1, d), y)


def _dispatch_tables(route, rows):
    n = route.shape[0]
    a = n * TOP_K_INNER
    n_max = a // rows + N_EXPERTS
    e_flat = route[:, :TOP_K_INNER].astype(jnp.int32).reshape(a)
    order = jnp.argsort(e_flat, stable=True).astype(jnp.int32)
    sorted_pos = jnp.argsort(order).astype(jnp.int32)
    counts = jnp.sum((e_flat[:, None] == jnp.arange(N_EXPERTS, dtype=jnp.int32)[None, :]).astype(jnp.int32), axis=0)
    seg_start = jnp.cumsum(counts) - counts
    n_chunks = (counts + rows - 1) // rows
    chunk_end = jnp.cumsum(n_chunks)
    chunk_base = chunk_end - n_chunks
    n_used = chunk_end[-1].astype(jnp.int32)
    cidx = jnp.arange(n_max, dtype=jnp.int32)
    ce = jnp.minimum(jnp.searchsorted(chunk_end, cidx, side='right'), N_EXPERTS - 1).astype(jnp.int32)
    ce = jnp.where(cidx < n_used, ce, ce[n_used - 1])
    ustart = seg_start[ce] + (cidx - chunk_base[ce]) * rows
    tok_sorted = jnp.concatenate([order // TOP_K_INNER, jnp.zeros((rows,), jnp.int32)])
    tok_chunks = jax.vmap(lambda s0: lax.dynamic_slice(tok_sorted, (s0,), (rows,)))(jnp.minimum(ustart, a))
    pos = chunk_base[e_flat] * rows + (sorted_pos - seg_start[e_flat])
    valid = jnp.clip(counts[ce] - (cidx - chunk_base[ce]) * rows, 0, rows)
    valid = jnp.concatenate([jnp.where(cidx < n_used, valid, 0), jnp.zeros((2,), jnp.int32)]).astype(jnp.int32)
    return tok_chunks.astype(jnp.int32), ce, valid, n_used, pos.astype(jnp.int32)


def _moe(x, layer, ffn_norm, moe_w_grp, moe_b_grp, moe_w_sub, moe_b_sub, moe_w_gate, moe_w_up, moe_w_down,
         g_next, *, n_first, last):
    n, d = x.shape
    w_r = jnp.concatenate([moe_w_grp[layer],
                           jnp.transpose(moe_w_sub[layer], (1, 0, 2)).reshape(d, N_EXPERTS)], axis=1)
    w_r = jnp.pad(w_r, ((0, 0), (0, ROUTE_LANES - w_r.shape[1])))
    w_hi = w_r.astype(BF16)
    w_lo = (w_r - w_hi.astype(F32)).astype(BF16)
    b_r = jnp.concatenate([moe_b_grp[layer], moe_b_sub[layer].reshape(N_EXPERTS)])
    b_r = jnp.pad(b_r, (0, ROUTE_LANES - b_r.shape[0])).reshape(1, ROUTE_LANES).astype(F32)

    hp, route = _router(x, ffn_norm[layer], w_hi, w_lo, b_r)
    rows = _tile(n * TOP_K_INNER, EXPERT_ROWS)
    tok_chunks, ce, valid, n_used, pos = _dispatch_tables(route, rows)
    y = _experts(hp, tok_chunks, ce, valid, n_used, moe_w_gate, moe_w_up, moe_w_down, layer)
    return _combine(x, y, pos, route, g_next, n_first=n_first, last=last)


def kernel(x_prompt, x_sample, mix_norm, ffn_norm, final_norm, cv_w_pw1, cv_b_pw1, cv_w_dw, cv_b_dw, cv_ln_g, cv_ln_b, cv_w_pw2, cv_b_pw2, na_w_qkv, na_rpb, na_w_o, moe_w_grp, moe_b_grp, moe_w_sub, moe_b_sub, moe_w_gate, moe_w_up, moe_w_down):
    bp, seq, d = x_prompt.shape
    bs = x_sample.shape[0]
    assert x_sample.shape[1:] == (seq, d) and seq % GRID_W == 0
    n_p = bp * seq
    xp, xs = x_prompt.reshape(n_p, d), x_sample.reshape(bs * seq, d)
    moe_p = (ffn_norm, moe_w_grp, moe_b_grp, moe_w_sub, moe_b_sub, moe_w_gate, moe_w_up, moe_w_down)

    h = _rmsnorm_stacked(xp, xs, mix_norm[0], BF16)
    u = _glu_matmul(h, cv_w_pw1[0].astype(BF16), cv_b_pw1[0])
    v = _conv_ln_swish(u, cv_w_dw[0], cv_b_dw[0], cv_ln_g[0], cv_ln_b[0], seq)
    x = _matmul_bias_res_stacked(v, cv_w_pw2[0].astype(BF16), cv_b_pw2[0], xp, xs, name="pw2_res")
    x, h = _moe(x, 0, *moe_p, mix_norm[1], n_first=n_p, last=False)

    qkv = _matmul(h, na_w_qkv[0].astype(BF16), out_dtype=BF16, name="qkv")
    o = _natten(qkv, na_rpb[0], seq)
    x = _matmul(o, na_w_o[0].astype(BF16), res=x, name="wo_res")
    yp, ys = _moe(x, 1, *moe_p, final_norm, n_first=n_p, last=True)
    return (yp.reshape(bp, seq, d), ys.reshape(bs, seq, d))
```

```python
import functools

import numpy as np
import jax
import jax.numpy as jnp
from jax import lax
from jax.experimental import pallas as pl
from jax.experimental.pallas import tpu as pltpu

GRID_W = 64
CONV_WIDTH = 31
NA_HEAD_DIM = 128
WIN_H = 8
WIN_W = 16
N_GROUPS = 8
EXPERTS_PER_GROUP = 8
N_EXPERTS = N_GROUPS * EXPERTS_PER_GROUP
TOP_K_INNER = 2
RMS_EPS = 1e-6
LN_EPS = 1e-5
NEG_INF = -1e30
LOG2_E = 1.4426950408889634

LANES = 128
BF16_SUBLANES = 16
V7X_VMEM_BUDGET = 58 * 1024 * 1024

MM_TM = 1024
MM_TN = 512
MM_TN_WIDE = 1024
NORM_TM = 256
CONV_TS = 256
CONV_HALO = BF16_SUBLANES
CONV_RB = 64
LN_RB = 64
NA_QROWS = 4
NA_KROWS = NA_QROWS + WIN_H
NA_UNROLL = 4
ROUTER_TM = 512
ROUTE_LANES = LANES
EXPERT_ROWS = 1024
EXPERT_SUB = 256
EXPERT_TF = 256
EXPERT_TN = 512
COMBINE_TC = 256

F32 = jnp.float32
BF16 = jnp.bfloat16


def _tile(n, pref):
    t = min(n, pref)
    assert n % t == 0, (n, pref)
    return t


def _params(*sem):
    return pltpu.CompilerParams(dimension_semantics=sem, vmem_limit_bytes=V7X_VMEM_BUDGET)


def _first_blocks(nb_first):
    return (lambda i: (jnp.minimum(i, nb_first - 1), 0)), (lambda i: (jnp.maximum(i - nb_first, 0), 0))


def _rmsnorm_kernel(xa_ref, xb_ref, g_ref, o_ref, *, nb_first):
    def norm(x_ref):
        x = x_ref[...]
        ms = jnp.mean(x * x, axis=-1, keepdims=True)
        o_ref[...] = (x * lax.rsqrt(ms + RMS_EPS) * g_ref[...]).astype(o_ref.dtype)

    i = pl.program_id(0)
    pl.when(i < nb_first)(lambda: norm(xa_ref))
    pl.when(i >= nb_first)(lambda: norm(xb_ref))


def _rmsnorm_stacked(xa, xb, g, out_dtype):
    na, d = xa.shape
    n = na + xb.shape[0]
    tm = _tile(na, NORM_TM)
    assert xb.shape[0] % tm == 0
    a_map, b_map = _first_blocks(na // tm)
    return pl.pallas_call(
        functools.partial(_rmsnorm_kernel, nb_first=na // tm),
        grid=(n // tm,),
        in_specs=[pl.BlockSpec((tm, d), a_map),
                  pl.BlockSpec((tm, d), b_map),
                  pl.BlockSpec((1, d), lambda i: (0, 0))],
        out_specs=pl.BlockSpec((tm, d), lambda i: (i, 0)),
        out_shape=jax.ShapeDtypeStruct((n, d), out_dtype),
        compiler_params=_params("arbitrary"),
        name="rmsnorm",
    )(xa, xb, g.reshape(1, d))


def _glu_kernel(h_ref, wa_ref, wg_ref, ba_ref, bg_ref, o_ref):
    h = h_ref[...]
    a = jnp.dot(h, wa_ref[...], preferred_element_type=F32) + ba_ref[...]
    g = jnp.dot(h, wg_ref[...], preferred_element_type=F32) + bg_ref[...]
    o_ref[...] = (a * jax.nn.sigmoid(g)).astype(o_ref.dtype)


def _glu_matmul(h, w, b):
    n, k = h.shape
    d = w.shape[1] // 2
    tm, tn = _tile(n, MM_TM), _tile(d, MM_TN)
    nj = d // tn
    b2 = b.reshape(1, 2 * d)
    return pl.pallas_call(
        _glu_kernel,
        grid=(n // tm, nj),
        in_specs=[pl.BlockSpec((tm, k), lambda i, j: (i, 0)),
                  pl.BlockSpec((k, tn), lambda i, j: (0, j)),
                  pl.BlockSpec((k, tn), lambda i, j: (0, j + nj)),
                  pl.BlockSpec((1, tn), lambda i, j: (0, j)),
                  pl.BlockSpec((1, tn), lambda i, j: (0, j + nj))],
        out_specs=pl.BlockSpec((tm, tn), lambda i, j: (i, j)),
        out_shape=jax.ShapeDtypeStruct((n, d), BF16),
        compiler_params=_params("parallel", "arbitrary"),
        name="pw1_glu",
    )(h, w, w, b2, b2)


def _mm_kernel(a_ref, w_ref, o_ref):
    o_ref[...] = jnp.dot(a_ref[...], w_ref[...], preferred_element_type=F32).astype(o_ref.dtype)


def _mm_bias_res2_kernel(a_ref, w_ref, b_ref, ra_ref, rb_ref, o_ref, *, nb_first):
    y = jnp.dot(a_ref[...], w_ref[...], preferred_element_type=F32) + b_ref[...]
    i = pl.program_id(0)

    @pl.when(i < nb_first)
    def _():
        o_ref[...] = ra_ref[...] + y

    @pl.when(i >= nb_first)
    def _():
        o_ref[...] = rb_ref[...] + y


def _matmul_bias_res_stacked(a, w, bias, res_a, res_b, *, name):
    n, k = a.shape
    m = w.shape[1]
    tm, tn = _tile(res_a.shape[0], MM_TM), _tile(m, MM_TN)
    assert res_a.shape[0] + res_b.shape[0] == n and res_b.shape[0] % tm == 0
    nbf, nj = res_a.shape[0] // tm, m // tn
    o_spec = pl.BlockSpec((tm, tn), lambda i, j: (i, j))
    ra_spec = pl.BlockSpec((tm, tn), lambda i, j: (jnp.minimum(i, nbf - 1), jnp.where(i < nbf, j, nj - 1)))
    rb_spec = pl.BlockSpec((tm, tn), lambda i, j: (jnp.maximum(i - nbf, 0), jnp.where(i >= nbf, j, 0)))
    return pl.pallas_call(
        functools.partial(_mm_bias_res2_kernel, nb_first=nbf),
        grid=(n // tm, nj),
        in_specs=[pl.BlockSpec((tm, k), lambda i, j: (i, 0)),
                  pl.BlockSpec((k, tn), lambda i, j: (0, j)),
                  pl.BlockSpec((1, tn), lambda i, j: (0, j)),
                  ra_spec, rb_spec],
        out_specs=o_spec,
        out_shape=jax.ShapeDtypeStruct((n, m), F32),
        compiler_params=_params("arbitrary", "arbitrary"),
        name=name,
    )(a, w, bias.reshape(1, m), res_a, res_b)


def _mm_res_kernel(a_ref, w_ref, r_ref, o_ref):
    o_ref[...] = r_ref[...] + jnp.dot(a_ref[...], w_ref[...], preferred_element_type=F32)


def _matmul(a, w, *, res=None, out_dtype=F32, name="matmul"):
    n, k = a.shape
    m = w.shape[1]
    tm, tn = _tile(n, MM_TM), _tile(m, MM_TN_WIDE)
    a_spec = pl.BlockSpec((tm, k), lambda i, j: (i, 0))
    w_spec = pl.BlockSpec((k, tn), lambda i, j: (0, j))
    o_spec = pl.BlockSpec((tm, tn), lambda i, j: (i, j))
    if res is None:
        kern, specs, args = _mm_kernel, [a_spec, w_spec], (a, w)
    else:
        kern, specs, args = _mm_res_kernel, [a_spec, w_spec, o_spec], (a, w, res)
    return pl.pallas_call(
        kern,
        grid=(n // tm, m // tn),
        in_specs=specs,
        out_specs=o_spec,
        out_shape=jax.ShapeDtypeStruct((n, m), out_dtype),
        compiler_params=_params("parallel", "arbitrary"),
        name=name,
    )(*args)


def _conv_kernel(prev_ref, cur_ref, next_ref, w_ref, bdw_ref, g_ref, b_ref, o_ref,
                 win_ref, acc_ref, *, ts, n_tblk, d):
    i = pl.program_id(1)
    halo = CONV_HALO
    n_cb = d // LANES
    has_prev = i > 0
    has_next = i < n_tblk - 1
    for cb in range(n_cb):
        sl = slice(cb * LANES, (cb + 1) * LANES)
        win_ref[cb, 0:halo, :] = jnp.where(has_prev, prev_ref[:, sl].astype(F32), 0.0)
        win_ref[cb, halo:halo + ts, :] = cur_ref[:, sl].astype(F32)
        win_ref[cb, halo + ts:halo + ts + halo, :] = jnp.where(has_next, next_ref[:, sl].astype(F32), 0.0)

    base = halo - CONV_WIDTH // 2
    span = 2 * CONV_RB
    starts = [sp * span + par for sp in range(ts // span) for par in (0, 1)]

    def col_body(cb, carry):
        accs = [jnp.zeros((CONV_RB, LANES), F32) for _ in starts]
        for k in range(CONV_WIDTH):
            wk = w_ref[cb, k:k + 1, :]
            for j, st in enumerate(starts):
                accs[j] = accs[j] + win_ref[cb, pl.ds(st + base + k, CONV_RB, stride=2), :] * wk
        bias = bdw_ref[cb]
        for j, st in enumerate(starts):
            acc_ref[cb, pl.ds(st, CONV_RB, stride=2), :] = accs[j] + bias
        return carry

    lax.fori_loop(0, n_cb, col_body, 0)

    inv_d = 1.0 / d

    def ln_body(rb, carry):
        r0 = pl.multiple_of(rb * LN_RB, LN_RB)
        y = acc_ref[:, pl.ds(r0, LN_RB), :]
        mu = jnp.sum(jnp.sum(y, axis=0), axis=-1, keepdims=True) * inv_d
        yc = y - mu[None]
        var = jnp.sum(jnp.sum(yc * yc, axis=0), axis=-1, keepdims=True) * inv_d
        z = yc * lax.rsqrt(var + LN_EPS)[None] * g_ref[...] + b_ref[...]
        z = (z * jax.nn.sigmoid(z)).astype(o_ref.dtype)
        for cb in range(n_cb):
            o_ref[pl.ds(r0, LN_RB), cb * LANES:(cb + 1) * LANES] = z[cb]
        return carry

    lax.fori_loop(0, ts // LN_RB, ln_body, 0)


def _conv_ln_swish(u, w_dw, b_dw, ln_g, ln_b, seq):
    n, d = u.shape
    n_seq = n // seq
    ts = _tile(seq, CONV_TS)
    n_tblk = seq // ts
    halo = CONV_HALO
    n_cb = d // LANES
    assert halo >= CONV_WIDTH // 2 and ts % halo == 0 and ts % (2 * CONV_RB) == 0 and d % LANES == 0
    hb = ts // halo
    last_hblk = n // halo - 1

    def prev_map(b, i):
        return (jnp.maximum((b * n_tblk + i) * hb - 1, 0), 0)

    def next_map(b, i):
        return (jnp.minimum((b * n_tblk + i + 1) * hb, last_hblk), 0)

    slab = lambda v: v.reshape(-1, n_cb, LANES).transpose(1, 0, 2)
    vec = lambda b, i: (0, 0, 0)
    kern = functools.partial(_conv_kernel, ts=ts, n_tblk=n_tblk, d=d)
    return pl.pallas_call(
        kern,
        grid=(n_seq, n_tblk),
        in_specs=[pl.BlockSpec((halo, d), prev_map),
                  pl.BlockSpec((ts, d), lambda b, i: (b * n_tblk + i, 0)),
                  pl.BlockSpec((halo, d), next_map),
                  pl.BlockSpec((n_cb, CONV_WIDTH, LANES), vec),
                  pl.BlockSpec((n_cb, 1, LANES), vec),
                  pl.BlockSpec((n_cb, 1, LANES), vec),
                  pl.BlockSpec((n_cb, 1, LANES), vec)],
        out_specs=pl.BlockSpec((ts, d), lambda b, i: (b * n_tblk + i, 0)),
        out_shape=jax.ShapeDtypeStruct((n, d), BF16),
        scratch_shapes=[pltpu.VMEM((n_cb, ts + 2 * halo, LANES), F32),
                        pltpu.VMEM((n_cb, ts, LANES), F32)],
        compiler_params=_params("parallel", "arbitrary"),
        name="dwconv_ln_swish",
    )(u, u, u, slab(w_dw), slab(b_dw), slab(ln_g), slab(ln_b))


def _natten_plan(rows):
    kh = min(WIN_H, rows)
    assert rows % NA_QROWS == 0 and rows >= NA_KROWS
    n_groups = rows // NA_QROWS
    k_start = np.clip(np.arange(n_groups) * NA_QROWS - kh // 2, 0, rows - NA_KROWS)
    patterns, slab_of = [], []
    for g in range(n_groups):
        pat = np.full((NA_QROWS, NA_KROWS), -1, np.int64)
        for qi in range(NA_QROWS):
            r = g * NA_QROWS + qi
            rs = int(np.clip(r - kh // 2, 0, rows - kh))
            for j in range(NA_KROWS):
                kr = int(k_start[g]) + j
                if rs <= kr < rs + kh:
                    pat[qi, j] = kr - r + WIN_H - 1
            assert (pat[qi] >= 0).sum() == kh
        key = pat.tobytes()
        if key not in [p.tobytes() for p in patterns]:
            patterns.append(pat)
        slab_of.append([p.tobytes() for p in patterns].index(key))
    return k_start.astype(np.int32), np.asarray(slab_of, np.int32), np.stack(patterns)


def _natten_bias(rpb, patterns):
    w = GRID_W
    qc = np.arange(w)
    cs = np.clip(qc - WIN_W // 2, 0, w - WIN_W)
    kc = np.arange(w)
    col_mask = (kc[None, :] >= cs[:, None]) & (kc[None, :] < cs[:, None] + WIN_W)
    col_idx = np.clip(kc[None, :] - qc[:, None] + WIN_W - 1, 0, 2 * WIN_W - 2)
    heads = rpb.shape[0]
    blocks = jnp.where(jnp.asarray(col_mask)[None, None], rpb.astype(F32)[:, :, col_idx] * LOG2_E, NEG_INF)
    masked = jnp.full((heads, w, w), NEG_INF, F32)
    slabs = []
    for pat in patterns:
        rows_ = [jnp.concatenate([blocks[:, int(ri)] if ri >= 0 else masked for ri in pat_q], axis=-1)
                 for pat_q in pat]
        slabs.append(jnp.concatenate(rows_, axis=-2))
    return jnp.stack(slabs, axis=1)


def _natten_kernel(kstart_ref, slab_ref, q_ref, k_ref, v_ref, bias_ref, o_ref, *, n_groups, scale):
    gq = NA_QROWS * GRID_W
    gk = NA_KROWS * GRID_W

    def body(g, carry):
        q0 = pl.multiple_of(g * gq, gq)
        k0 = pl.multiple_of(kstart_ref[g] * GRID_W, GRID_W)
        q = q_ref[pl.ds(q0, gq), :]
        kb = k_ref[pl.ds(k0, gk), :]
        vb = v_ref[pl.ds(k0, gk), :]
        s = lax.dot_general(q, kb, (((1,), (1,)), ((), ())), preferred_element_type=F32) * (scale * LOG2_E)
        s = s + bias_ref[0, slab_ref[g]]
        m = jnp.max(s, axis=-1, keepdims=True)
        e = jnp.exp2(s - m)
        inv = 1.0 / jnp.sum(e, axis=-1, keepdims=True)
        o = jnp.dot(e.astype(BF16), vb, preferred_element_type=F32) * inv
        o_ref[pl.ds(q0, gq), :] = o.astype(o_ref.dtype)
        return carry

    lax.fori_loop(0, n_groups, body, 0, unroll=NA_UNROLL)


def _natten(qkv, rpb, seq):
    n, d3 = qkv.shape
    d = d3 // 3
    heads = d // NA_HEAD_DIM
    n_seq = n // seq
    rows = seq // GRID_W
    k_start, slab_of, patterns = _natten_plan(rows)
    bias = _natten_bias(rpb, patterns)
    n_slabs = patterns.shape[0]
    gq, gk = NA_QROWS * GRID_W, NA_KROWS * GRID_W
    kern = functools.partial(_natten_kernel, n_groups=rows // NA_QROWS, scale=NA_HEAD_DIM ** -0.5)
    grid_spec = pltpu.PrefetchScalarGridSpec(
        num_scalar_prefetch=2,
        grid=(heads, n_seq),
        in_specs=[pl.BlockSpec((seq, NA_HEAD_DIM), lambda h, b, ks, sl: (b, h)),
                  pl.BlockSpec((seq, NA_HEAD_DIM), lambda h, b, ks, sl: (b, heads + h)),
                  pl.BlockSpec((seq, NA_HEAD_DIM), lambda h, b, ks, sl: (b, 2 * heads + h)),
                  pl.BlockSpec((1, n_slabs, gq, gk), lambda h, b, ks, sl: (h, 0, 0, 0))],
        out_specs=pl.BlockSpec((seq, NA_HEAD_DIM), lambda h, b, ks, sl: (b, h)),
    )
    return pl.pallas_call(
        kern,
        grid_spec=grid_spec,
        out_shape=jax.ShapeDtypeStruct((n, d), BF16),
        compiler_params=_params("parallel", "arbitrary"),
        name="natten",
    )(jnp.asarray(k_start), jnp.asarray(slab_of), qkv, qkv, qkv, bias)


def _router_kernel(x_ref, g_ref, whi_ref, wlo_ref, b_ref, hp_ref, route_ref, *, d):
    x = x_ref[...]
    ms = jnp.mean(x * x, axis=-1, keepdims=True)
    hn = x * lax.rsqrt(ms + RMS_EPS) * g_ref[...]
    hb = hn.astype(BF16)
    hbf = hb.astype(F32)
    hlo = (hn - hbf).astype(BF16)
    logits = (jnp.dot(hb, whi_ref[...], preferred_element_type=F32)
              + (jnp.dot(hb, wlo_ref[...], preferred_element_type=F32)
                 + jnp.dot(hlo, whi_ref[...], preferred_element_type=F32))) + b_ref[...]

    lane = lax.broadcasted_iota(jnp.int32, logits.shape, 1)
    big = jnp.int32(ROUTE_LANES)
    is_grp = lane < N_GROUPS
    gl = jnp.where(is_grp, logits, NEG_INF)
    gmax = jnp.max(gl, axis=-1, keepdims=True)
    gsum = jnp.sum(jnp.where(is_grp, jnp.exp(gl - gmax), 0.0), axis=-1, keepdims=True)
    p_top = 1.0 / gsum
    g_top = jnp.min(jnp.where(is_grp & (gl == gmax), lane, big), axis=-1, keepdims=True)
    lo = N_GROUPS + g_top * EXPERTS_PER_GROUP
    in_grp = (lane >= lo) & (lane < lo + EXPERTS_PER_GROUP)
    sl = jnp.where(in_grp, logits, NEG_INF)
    v1 = jnp.max(sl, axis=-1, keepdims=True)
    i1 = jnp.min(jnp.where(in_grp & (sl == v1), lane, big), axis=-1, keepdims=True)
    rest = in_grp & (lane != i1)
    sl2 = jnp.where(rest, logits, NEG_INF)
    v2 = jnp.max(sl2, axis=-1, keepdims=True)
    i2 = jnp.min(jnp.where(rest & (sl2 == v2), lane, big), axis=-1, keepdims=True)
    ex = jnp.exp(v2 - v1)
    den = 1.0 + ex
    gate1 = p_top * (1.0 / den)
    gate2 = p_top * (ex / den)
    e1 = (i1 - N_GROUPS).astype(F32)
    e2 = (i2 - N_GROUPS).astype(F32)
    route_ref[...] = jnp.where(lane == 0, e1,
                               jnp.where(lane == 1, e2,
                                         jnp.where(lane == 2, gate1,
                                                   jnp.where(lane == 3, gate2, 0.0))))

    bits = lax.bitcast_convert_type(hbf, jnp.uint32)
    half = d // 2
    hp_ref[...] = bits[:, :half] | (bits[:, half:] >> 16)


def _router(x, g, w_hi, w_lo, b):
    n, d = x.shape
    tm = _tile(n, ROUTER_TM)
    kern = functools.partial(_router_kernel, d=d)
    vec = lambda i: (0, 0)
    return pl.pallas_call(
        kern,
        grid=(n // tm,),
        in_specs=[pl.BlockSpec((tm, d), lambda i: (i, 0)),
                  pl.BlockSpec((1, d), vec),
                  pl.BlockSpec((d, ROUTE_LANES), vec),
                  pl.BlockSpec((d, ROUTE_LANES), vec),
                  pl.BlockSpec((1, ROUTE_LANES), vec)],
        out_specs=[pl.BlockSpec((tm, d // 2), lambda i: (i, 0)),
                   pl.BlockSpec((tm, ROUTE_LANES), lambda i: (i, 0))],
        out_shape=[jax.ShapeDtypeStruct((n, d // 2), jnp.uint32),
                   jax.ShapeDtypeStruct((n, ROUTE_LANES), F32)],
        compiler_params=_params("parallel"),
        name="moe_router",
    )(x, g.reshape(1, d), w_hi, w_lo, b)


def _row_copy(src_ref, src_row, dst_ref, dst_row, sem):
    return pltpu.make_async_copy(src_ref.at[pl.ds(src_row, 1)], dst_ref.at[pl.ds(dst_row, 1)], sem)


def _pack_bf16_pair(hi, lo):
    hi_bits = lax.bitcast_convert_type(hi.astype(BF16).astype(F32), jnp.uint32)
    lo_bits = lax.bitcast_convert_type(lo.astype(BF16).astype(F32), jnp.uint32)
    return hi_bits | (lo_bits >> 16)


def _unpack_bf16_pair(w):
    return (lax.bitcast_convert_type(w & jnp.uint32(0xFFFF0000), F32),
            lax.bitcast_convert_type(w << 16, F32))


def _expert_kernel(ce_ref, nv_ref, nu_ref, tok_ref, hp_ref, wg_ref, wu_ref, wda_ref, wdb_ref, y_ref,
                   xraw_ref, xb_ref, h_ref, sem, *, rows, n_f, tf, n_chunks):
    c = pl.program_id(0)
    s = pl.program_id(1)
    sub, sub2 = EXPERT_SUB, 2 * EXPERT_SUB
    half = xraw_ref.shape[1]
    blocks_of = lambda k: (nv_ref[jnp.maximum(k, 0)] + sub - 1) // sub
    n_blk = jnp.where(c >= 1, blocks_of(c - 1), 0)
    n_blk_next = blocks_of(c)
    n_blk_wait = jnp.where(c >= 1, jnp.maximum(jnp.where(c >= 2, blocks_of(c - 2), 0), blocks_of(c - 1)), 0)
    rows_per_step = sub // n_f

    def gather_rows(i):
        r0 = i * sub + s * rows_per_step
        for j in range(rows_per_step):
            _row_copy(hp_ref, tok_ref[0, 0, r0 + j], xraw_ref, r0 + j, sem).start()

    @pl.when(s == 0)
    def _():
        def wait_block(i, carry):
            r0 = pl.multiple_of(i * sub, sub)
            pltpu.make_async_copy(hp_ref.at[pl.ds(0, sub)], xraw_ref.at[pl.ds(r0, sub)], sem).wait()
            return carry

        def unpack_block(i, carry):
            r0 = pl.multiple_of(i * sub, sub)
            hi, lo = _unpack_bf16_pair(xraw_ref[pl.ds(r0, sub), :])
            xb_ref[pl.ds(r0, sub), :half] = hi.astype(BF16)
            xb_ref[pl.ds(r0, sub), half:] = lo.astype(BF16)
            return carry

        lax.fori_loop(0, n_blk_wait, wait_block, 0)
        lax.fori_loop(0, n_blk, unpack_block, 0)

    @pl.when((s < n_f) & (n_blk > 0))
    def _():
        col = pl.multiple_of(s * tf, tf)

        def block(i, carry):
            r0 = pl.multiple_of(i * sub, sub)
            xb = xb_ref[pl.ds(r0, sub), :]
            g = jnp.dot(xb, wg_ref[0, 0].astype(BF16), preferred_element_type=F32)
            u = jnp.dot(xb, wu_ref[0, 0].astype(BF16), preferred_element_type=F32)
            act = (g * jax.nn.sigmoid(g) * u).astype(BF16)
            gather_rows(i)
            h_ref[pl.ds(r0, sub), pl.ds(col, tf)] = act
            return carry

        lax.fori_loop(0, n_blk, block, 0)

        @pl.when(n_blk % 2 == 1)
        def _():
            r0 = pl.multiple_of(n_blk * sub, sub)
            h_ref[pl.ds(r0, sub), pl.ds(col, tf)] = jnp.zeros((sub, tf), BF16)

    @pl.when(s < n_f)
    def _():
        def tail(i, carry):
            gather_rows(i)
            return carry

        lax.fori_loop(n_blk, jnp.maximum(n_blk, n_blk_next), tail, 0)

    @pl.when((c >= 1) & (c <= n_chunks) & (s >= n_f))
    def _():
        n_blk2 = (n_blk + 1) // 2

        def block(i, carry):
            r0 = pl.multiple_of(i * sub2, sub2)
            hb = h_ref[pl.ds(r0, sub2), :]
            ya = jnp.dot(hb, wda_ref[0, 0].astype(BF16), preferred_element_type=F32)
            yb = jnp.dot(hb, wdb_ref[0, 0].astype(BF16), preferred_element_type=F32)
            y_ref[pl.ds(r0, sub2), :] = _pack_bf16_pair(ya, yb)
            return carry

        def zero_block(i, carry):
            r0 = pl.multiple_of(i * sub2, sub2)
            y_ref[pl.ds(r0, sub2), :] = jnp.zeros((sub2, y_ref.shape[1]), jnp.uint32)
            return carry

        lax.fori_loop(0, n_blk2, block, 0)
        lax.fori_loop(n_blk2, rows // sub2, zero_block, 0)


def _experts(hp, tok_chunks, chunk_expert, chunk_valid, n_used, w_gate, w_up, w_down, layer):
    n, half = hp.shape
    d = 2 * half
    ff = w_gate.shape[-1]
    n_max, rows = tok_chunks.shape
    tf, tn = _tile(ff, EXPERT_TF), _tile(half, EXPERT_TN)
    n_f, n_n = ff // tf, half // tn
    assert rows % (2 * EXPERT_SUB) == 0 and EXPERT_SUB % n_f == 0 and chunk_valid.shape == (n_max + 2,)

    def chunk(c, nu):
        return jnp.clip(c - 1, 0, nu[0] - 1)

    def active(c, nu):
        return (c >= 1) & (c - 1 < nu[0])

    def gate_map(c, s, ce, nv, nu):
        nxt = jnp.clip(c, 0, nu[0] - 1)
        in_p1 = active(c, nu) & (s < n_f)
        has_next = c < nu[0]
        e = jnp.where(in_p1, ce[chunk(c, nu)], jnp.where(has_next, ce[nxt], ce[chunk(c, nu)]))
        t = jnp.where(in_p1, s, jnp.where(has_next, 0, n_f - 1))
        return (layer, e, 0, t)

    def down_tile(c, s, nu):
        return jnp.where(active(c, nu), jnp.maximum(s - n_f, 0), jnp.where(c == 0, 0, n_n - 1))

    def down_map_a(c, s, ce, nv, nu):
        return (layer, ce[chunk(c, nu)], 0, down_tile(c, s, nu))

    def down_map_b(c, s, ce, nv, nu):
        return (layer, ce[chunk(c, nu)], 0, n_n + down_tile(c, s, nu))

    def out_map(c, s, ce, nv, nu):
        t = jnp.where(c == 0, 0, jnp.where(c > n_max, n_n - 1, jnp.maximum(s - n_f, 0)))
        return (jnp.clip(c - 1, 0, n_max - 1), t)

    kern = functools.partial(_expert_kernel, rows=rows, n_f=n_f, tf=tf, n_chunks=n_max)
    grid_spec = pltpu.PrefetchScalarGridSpec(
        num_scalar_prefetch=3,
        grid=(n_max + 2, n_f + n_n),
        in_specs=[pl.BlockSpec((1, 1, rows), lambda c, s, ce, nv, nu: (jnp.minimum(c, n_max - 1), 0, 0),
                               memory_space=pltpu.SMEM),
                  pl.BlockSpec(memory_space=pl.ANY),
                  pl.BlockSpec((1, 1, d, tf), gate_map),
                  pl.BlockSpec((1, 1, d, tf), gate_map),
                  pl.BlockSpec((1, 1, ff, tn), down_map_a),
                  pl.BlockSpec((1, 1, ff, tn), down_map_b)],
        out_specs=pl.BlockSpec((rows, tn), out_map),
        scratch_shapes=[pltpu.VMEM((rows, half), jnp.uint32),
                        pltpu.VMEM((rows, d), BF16),
                        pltpu.VMEM((rows, ff), BF16),
                        pltpu.SemaphoreType.DMA(())],
    )
    return pl.pallas_call(
        kern,
        grid_spec=grid_spec,
        out_shape=jax.ShapeDtypeStruct((n_max * rows, half), jnp.uint32),
        compiler_params=_params("arbitrary", "arbitrary"),
        name="moe_experts",
    )(chunk_expert, chunk_valid, n_used.reshape(1), tok_chunks.reshape(n_max, 1, rows), hp, w_gate, w_up,
      w_down, w_down)


def _combine_kernel(pos_ref, posn_ref, x_ref, route_ref, gn_ref, y_ref, o0_ref, o1_ref, ybuf_ref, sems,
                    *, tc, nb, nb_first, last):
    i = pl.program_id(0)
    slot = i & 1

    def issue(p_ref, to_slot):
        def body(t, carry):
            for k in range(TOP_K_INNER):
                pltpu.make_async_copy(y_ref.at[pl.ds(p_ref[0, 0, TOP_K_INNER * t + k], 1)],
                                      ybuf_ref.at[to_slot, pl.ds(k * tc + t, 1)],
                                      sems.at[to_slot]).start(priority=k)
            return carry

        lax.fori_loop(0, tc, body, 0, unroll=8)

    @pl.when(i == 0)
    def _():
        issue(pos_ref, 0)

    @pl.when(i + 1 < nb)
    def _():
        issue(posn_ref, 1 - slot)

    pltpu.make_async_copy(y_ref.at[pl.ds(0, 2 * tc)], ybuf_ref.at[slot], sems.at[slot]).wait()

    route = route_ref[...]
    yb = ybuf_ref.at[slot]
    half = yb.shape[1]
    hi0, lo0 = _unpack_bf16_pair(yb[0:tc, :])
    hi1, lo1 = _unpack_bf16_pair(yb[tc:2 * tc, :])
    g0, g1 = route[:, 2:3], route[:, 3:4]
    x = jnp.concatenate([x_ref[:, :half] + (g0 * hi0 + g1 * hi1),
                         x_ref[:, half:] + (g0 * lo0 + g1 * lo1)], axis=1)
    ms = jnp.mean(x * x, axis=-1, keepdims=True)
    hn = x * lax.rsqrt(ms + RMS_EPS) * gn_ref[...]
    if last:
        @pl.when(i < nb_first)
        def _():
            o0_ref[...] = hn

        @pl.when(i >= nb_first)
        def _():
            o1_ref[...] = hn
    else:
        o0_ref[...] = x
        o1_ref[...] = hn.astype(o1_ref.dtype)


def _combine(x, y, pos, route, g_next, *, n_first, last):
    n, d = x.shape
    tc = _tile(n, COMBINE_TC)
    nb = n // tc
    assert n_first % tc == 0
    nb_first = n_first // tc
    kern = functools.partial(_combine_kernel, tc=tc, nb=nb, nb_first=nb_first, last=last)
    row = pl.BlockSpec((tc, d), lambda i: (i, 0))
    if last:
        out_specs = [pl.BlockSpec((tc, d), lambda i: (jnp.minimum(i, nb_first - 1), 0)),
                     pl.BlockSpec((tc, d), lambda i: (jnp.maximum(i - nb_first, 0), 0))]
        out_shape = [jax.ShapeDtypeStruct((n_first, d), F32), jax.ShapeDtypeStruct((n - n_first, d), F32)]
    else:
        out_specs = [row, row]
        out_shape = [jax.ShapeDtypeStruct((n, d), F32), jax.ShapeDtypeStruct((n, d), BF16)]
    pos3 = pos.reshape(nb, 1, 2 * tc)
    return pl.pallas_call(
        kern,
        grid=(nb,),
        in_specs=[pl.BlockSpec((1, 1, 2 * tc), lambda i: (i, 0, 0), memory_space=pltpu.SMEM),
                  pl.BlockSpec((1, 1, 2 * tc), lambda i: (jnp.minimum(i + 1, nb - 1), 0, 0),
                               memory_space=pltpu.SMEM),
                  row,
                  pl.BlockSpec((tc, ROUTE_LANES), lambda i: (i, 0)),
                  pl.BlockSpec((1, d), lambda i: (0, 0)),
                  pl.BlockSpec(memory_space=pl.ANY)],
        out_specs=out_specs,
        out_shape=out_shape,
        scratch_shapes=[pltpu.VMEM((2, 2 * tc, d // 2), jnp.uint32), pltpu.SemaphoreType.DMA((2,))],
        compiler_params=_params("arbitrary"),
        name="moe_combine",
    )(pos3, pos3, x, route, g_next.reshape(1, d), y)


def _dispatch_tables(route, rows):
    n = route.shape[0]
    a = n * TOP_K_INNER
    n_max = a // rows + N_EXPERTS
    e_flat = route[:, :TOP_K_INNER].astype(jnp.int32).reshape(a)
    order = jnp.argsort(e_flat, stable=True).astype(jnp.int32)
    sorted_pos = jnp.argsort(order).astype(jnp.int32)
    counts = jnp.sum((e_flat[:, None] == jnp.arange(N_EXPERTS, dtype=jnp.int32)[None, :]).astype(jnp.int32), axis=0)
    seg_start = jnp.cumsum(counts) - counts
    n_chunks = (counts + rows - 1) // rows
    chunk_end = jnp.cumsum(n_chunks)
    chunk_base = chunk_end - n_chunks
    n_used = chunk_end[-1].astype(jnp.int32)
    cidx = jnp.arange(n_max, dtype=jnp.int32)
    ce = jnp.minimum(jnp.searchsorted(chunk_end, cidx, side='right'), N_EXPERTS - 1).astype(jnp.int32)
    ce = jnp.where(cidx < n_used, ce, ce[n_used - 1])
    ustart = seg_start[ce] + (cidx - chunk_base[ce]) * rows
    tok_sorted = jnp.concatenate([order // TOP_K_INNER, jnp.zeros((rows,), jnp.int32)])
    tok_chunks = jax.vmap(lambda s0: lax.dynamic_slice(tok_sorted, (s0,), (rows,)))(jnp.minimum(ustart, a))
    pos = chunk_base[e_flat] * rows + (sorted_pos - seg_start[e_flat])
    valid = jnp.clip(counts[ce] - (cidx - chunk_base[ce]) * rows, 0, rows)
    valid = jnp.concatenate([jnp.where(cidx < n_used, valid, 0), jnp.zeros((2,), jnp.int32)]).astype(jnp.int32)
    return tok_chunks.astype(jnp.int32), ce, valid, n_used, pos.astype(jnp.int32)


def _moe(x, layer, ffn_norm, moe_w_grp, moe_b_grp, moe_w_sub, moe_b_sub, moe_w_gate, moe_w_up, moe_w_down,
         g_next, *, n_first, last):
    n, d = x.shape
    w_r = jnp.concatenate([moe_w_grp[layer],
                           jnp.transpose(moe_w_sub[layer], (1, 0, 2)).reshape(d, N_EXPERTS)], axis=1)
    w_r = jnp.pad(w_r, ((0, 0), (0, ROUTE_LANES - w_r.shape[1])))
    w_hi = w_r.astype(BF16)
    w_lo = (w_r - w_hi.astype(F32)).astype(BF16)
    b_r = jnp.concatenate([moe_b_grp[layer], moe_b_sub[layer].reshape(N_EXPERTS)])
    b_r = jnp.pad(b_r, (0, ROUTE_LANES - b_r.shape[0])).reshape(1, ROUTE_LANES).astype(F32)

    hp, route = _router(x, ffn_norm[layer], w_hi, w_lo, b_r)
    rows = _tile(n * TOP_K_INNER, EXPERT_ROWS)
    tok_chunks, ce, valid, n_used, pos = _dispatch_tables(route, rows)
    y = _experts(hp, tok_chunks, ce, valid, n_used, moe_w_gate, moe_w_up, moe_w_down, layer)
    return _combine(x, y, pos, route, g_next, n_first=n_first, last=last)


def kernel(x_prompt, x_sample, mix_norm, ffn_norm, final_norm, cv_w_pw1, cv_b_pw1, cv_w_dw, cv_b_dw, cv_ln_g, cv_ln_b, cv_w_pw2, cv_b_pw2, na_w_qkv, na_rpb, na_w_o, moe_w_grp, moe_b_grp, moe_w_sub, moe_b_sub, moe_w_gate, moe_w_up, moe_w_down):
    bp, seq, d = x_prompt.shape
    bs = x_sample.shape[0]
    assert x_sample.shape[1:] == (seq, d) and seq % GRID_W == 0
    n_p = bp * seq
    xp, xs = x_prompt.reshape(n_p, d), x_sample.reshape(bs * seq, d)
    moe_p = (ffn_norm, moe_w_grp, moe_b_grp, moe_w_sub, moe_b_sub, moe_w_gate, moe_w_up, moe_w_down)

    h = _rmsnorm_stacked(xp, xs, mix_norm[0], BF16)
    u = _glu_matmul(h, cv_w_pw1[0].astype(BF16), cv_b_pw1[0])
    v = _conv_ln_swish(u, cv_w_dw[0], cv_b_dw[0], cv_ln_g[0], cv_ln_b[0], seq)
    x = _matmul_bias_res_stacked(v, cv_w_pw2[0].astype(BF16), cv_b_pw2[0], xp, xs, name="pw2_res")
    x, h = _moe(x, 0, *moe_p, mix_norm[1], n_first=n_p, last=False)

    qkv = _matmul(h, na_w_qkv[0].astype(BF16), out_dtype=BF16, name="qkv")
    o = _natten(qkv, na_rpb[0], seq)
    x = _matmul(o, na_w_o[0].astype(BF16), res=x, name="wo_res")
    yp, ys = _moe(x, 1, *moe_p, final_norm, n_first=n_p, last=True)
    return (yp.reshape(bp, seq, d), ys.reshape(bs, seq, d))
```
